```python
import math, functools
import jax, jax.numpy as jnp
from jax import lax
import numpy as np

D_MODEL = 1024
BATCH = 16
SEQ = 256
DEPTH = 1
DEC_BATCH = 8
DEC_SEQ = 2048
PAST_LEN = 256

GRID_W = 64
GLA_HEADS = 4
GLA_DK = 128
GLA_DV = 256
GLA_QK = GLA_HEADS * GLA_DK
GLA_VW = GLA_HEADS * GLA_DV
GLA_RANK = 16
GLA_TAU = 16.0
CHUNK = 64
DIFF_HEADS = 8
DIFF_DH = 64
DIFF_QK = DIFF_HEADS * 2 * DIFF_DH
DIFF_VW = DIFF_HEADS * 2 * DIFF_DH
Q_BLOCK = 128
ROPE_BASE = 10000.0
ROPE_AXIS_DIM = DIFF_DH // 2
N_EXPERTS = 16
EXPERT_FF = 2816
CAPACITY_FACTOR = 2
EPS = 1e-6
IN_SIZES = (GLA_QK, GLA_QK, GLA_VW, GLA_VW, GLA_RANK, GLA_RANK, DIFF_QK, DIFF_QK, DIFF_VW, D_MODEL, D_MODEL)
IN_WIDTH = 2 * GLA_QK + 2 * GLA_VW + 2 * GLA_RANK + 2 * DIFF_QK + DIFF_VW + 2 * D_MODEL

kernel_name = 'hybrid_gla_diffattn_ec_dit_step'


def ln_plain(x):
    x32 = x.astype(jnp.float32)
    mu = jnp.mean(x32, axis=-1, keepdims=True)
    var = jnp.mean(jnp.square(x32 - mu), axis=-1, keepdims=True)
    return ((x32 - mu) * lax.rsqrt(var + EPS)).astype(x.dtype)


def ln_affine(x, g, b):
    x32 = x.astype(jnp.float32)
    mu = jnp.mean(x32, axis=-1, keepdims=True)
    var = jnp.mean(jnp.square(x32 - mu), axis=-1, keepdims=True)
    y = (x32 - mu) * lax.rsqrt(var + EPS) * g.astype(jnp.float32) + b.astype(jnp.float32)
    return y.astype(x.dtype)


def rms_norm(x, g):
    x32 = x.astype(jnp.float32)
    y = x32 * lax.rsqrt(jnp.mean(jnp.square(x32), axis=-1, keepdims=True) + EPS)
    return (y * g.astype(jnp.float32)).astype(x.dtype)


def grid_angles(n_tokens):
    rows = n_tokens // GRID_W
    pos_r = jnp.repeat(jnp.arange(rows, dtype=jnp.float32), GRID_W)
    pos_c = jnp.tile(jnp.arange(GRID_W, dtype=jnp.float32), rows)
    half = ROPE_AXIS_DIM // 2
    inv = ROPE_BASE ** (-jnp.arange(half, dtype=jnp.float32) / half)
    return pos_r[:, None] * inv, pos_c[:, None] * inv


def rope_half(x, ang):
    p = x.shape[-1] // 2
    cos = jnp.cos(ang)[None, :, None, None, :]
    sin = jnp.sin(ang)[None, :, None, None, :]
    x32 = x.astype(jnp.float32)
    x1, x2 = x32[..., :p], x32[..., p:]
    return jnp.concatenate([x1 * cos - x2 * sin, x1 * sin + x2 * cos], axis=-1).astype(x.dtype)


def axial_rope(x, ang_r, ang_c):
    return jnp.concatenate([rope_half(x[..., :ROPE_AXIS_DIM], ang_r),
                            rope_half(x[..., ROPE_AXIS_DIM:], ang_c)], axis=-1)


def gla_scan(q, k, v, log_a, s0):
    B, L, H, _ = q.shape
    dv = v.shape[-1]
    n = L // CHUNK

    def to_chunks(t):
        return jnp.moveaxis(t.astype(jnp.float32).reshape(B, n, CHUNK, H, t.shape[-1]), 1, 0)

    mask = jnp.tril(jnp.ones((CHUNK, CHUNK), dtype=bool))

    def step(S, inp):
        qc, kc, vc, gc = inp
        b = jnp.cumsum(gc, axis=1)
        b_last = b[:, -1]
        qe = qc * jnp.exp(b)
        ke = kc * jnp.exp(-b)
        o = jnp.einsum('bihk,bhkv->bihv', qe, S)
        A = jnp.where(mask, jnp.einsum('bihk,bjhk->bhij', qe, ke), 0.0)
        o = o + jnp.einsum('bhij,bjhv->bihv', A, vc)
        kd = kc * jnp.exp(b_last[:, None] - b)
        S = jnp.exp(b_last)[..., None] * S + jnp.einsum('bjhk,bjhv->bhkv', kd, vc)
        return S, o

    S, o = lax.scan(step, s0.astype(jnp.float32), (to_chunks(q), to_chunks(k), to_chunks(v), to_chunks(log_a)))
    o = jnp.moveaxis(o, 0, 1).reshape(B, L, H, dv)
    return o.astype(v.dtype), S


def diff_attention(q, k, v, lam):
    B, Lq, H, _, dh = q.shape
    nb = Lq // Q_BLOCK
    qb = jnp.moveaxis(q.reshape(B, nb, Q_BLOCK, H, 2, dh), 1, 0)
    scale = dh ** -0.5

    def block(qi):
        s = jnp.einsum('bqhcd,bkhcd->bhcqk', qi, k).astype(jnp.float32) * scale
        p = jax.nn.softmax(s, axis=-1)
        pd = (p[:, :, 0] - lam * p[:, :, 1]).astype(v.dtype)
        return jnp.einsum('bhqk,bkhe->bqhe', pd, v)

    o = lax.map(block, qb)
    return jnp.moveaxis(o, 0, 1).reshape(B, Lq, H, v.shape[-1])


def token_mixers(h, w_in, w_alpha_f, b_alpha_f, w_alpha_b, b_alpha_b, gla_norm_g, w_proj_a,
                 lam, lam_init, subln_g, w_proj_b, w_out, rope_ang, s0_f, s0_b, k_ctx, v_ctx):
    B, L, _ = h.shape
    offs, acc = [], 0
    for s in IN_SIZES[:-1]:
        acc += s
        offs.append(acc)
    qa, ka, va, ra, af, ab, qd, kd, vd, ga, gb = jnp.split(h @ w_in, offs, axis=-1)
    qa = qa.reshape(B, L, GLA_HEADS, GLA_DK) * (GLA_DK ** -0.5)
    ka = ka.reshape(B, L, GLA_HEADS, GLA_DK)
    va = va.reshape(B, L, GLA_HEADS, GLA_DV)
    log_f = (jax.nn.log_sigmoid((af @ w_alpha_f + b_alpha_f).astype(jnp.float32)) / GLA_TAU).reshape(B, L, GLA_HEADS, GLA_DK)
    log_b = (jax.nn.log_sigmoid((ab @ w_alpha_b + b_alpha_b).astype(jnp.float32)) / GLA_TAU).reshape(B, L, GLA_HEADS, GLA_DK)
    o_f, s_f = gla_scan(qa, ka, va, log_f, s0_f)
    o_b, s_b = gla_scan(jnp.flip(qa, 1), jnp.flip(ka, 1), jnp.flip(va, 1), jnp.flip(log_b, 1), s0_b)
    o_a = rms_norm(o_f + jnp.flip(o_b, 1), gla_norm_g) * jax.nn.silu(ra.reshape(B, L, GLA_HEADS, GLA_DV))
    o_a = o_a.reshape(B, L, GLA_VW)
    qd = qd.reshape(B, L, DIFF_HEADS, 2, DIFF_DH)
    kd = kd.reshape(B, L, DIFF_HEADS, 2, DIFF_DH)
    vd = vd.reshape(B, L, DIFF_HEADS, 2 * DIFF_DH)
    q_att, k_att, v_att = qd, kd, vd
    if rope_ang is not None:
        q_att = axial_rope(qd, rope_ang[0], rope_ang[1])
        k_att = axial_rope(kd, rope_ang[0], rope_ang[1])
    if k_ctx is not None:
        k_att = jnp.concatenate([k_att, k_ctx.astype(k_att.dtype)], axis=1)
        v_att = jnp.concatenate([v_att, v_ctx.astype(v_att.dtype)], axis=1)
    o_d = diff_attention(q_att, k_att, v_att, lam)
    o_d = (rms_norm(o_d, subln_g) * (1.0 - lam_init)).reshape(B, L, DIFF_VW)
    merged = jax.nn.sigmoid(ga) * (o_a @ w_proj_a) + jax.nn.sigmoid(gb) * (o_d @ w_proj_b)
    return merged @ w_out, kd, vd, s_f, s_b


def ec_moe(h, w_router, w_gate, w_up, w_down):
    B, L, D = h.shape
    n = B * L
    xt = h.reshape(n, D)
    aff = jax.nn.softmax((xt @ w_router).astype(jnp.float32), axis=-1)
    cap = CAPACITY_FACTOR * n // N_EXPERTS
    gate, idx = lax.top_k(aff.T, cap)
    xe = jnp.take(xt, idx, axis=0)
    hid = jax.nn.silu(jnp.einsum('ecd,edf->ecf', xe, w_gate)) * jnp.einsum('ecd,edf->ecf', xe, w_up)
    ye = jnp.einsum('ecf,efd->ecd', hid, w_down) * gate[..., None].astype(h.dtype)
    out = jnp.zeros_like(xt).at[idx.reshape(-1)].add(ye.reshape(-1, D))
    return out.reshape(B, L, D)


def trunk_layer(x, mod, mixer, moe, ln1_g, ln1_b, ln2_g, ln2_b, alpha, rope_ang, s0_f, s0_b, k_ctx, v_ctx):
    sh1, sc1, g1, sh2, sc2, g2 = jnp.split(mod[:, None, :].astype(x.dtype), 6, axis=-1)
    h = ln_plain(x) * (1 + sc1) + sh1
    mix, kd, vd, s_f, s_b = mixer(h, rope_ang=rope_ang, s0_f=s0_f, s0_b=s0_b, k_ctx=k_ctx, v_ctx=v_ctx)
    x = ln_affine(alpha * x + g1 * mix, ln1_g, ln1_b)
    h = ln_plain(x) * (1 + sc2) + sh2
    x = ln_affine(alpha * x + g2 * moe(h), ln2_g, ln2_b)
    return x, kd, vd, s_f, s_b


def setup_inputs(seed: int = 0) -> dict:
    key = jax.random.key(seed)
    ks = jax.random.split(key, 40)
    D = D_MODEL
    beta = (8.0 * DEPTH) ** -0.25

    def nrm(k, shape, s):
        return jax.random.normal(k, shape, jnp.float32) * s

    return {
        'x_prompt': nrm(ks[0], (BATCH, SEQ, D), 1.0),
        'x_sample': nrm(ks[1], (DEC_BATCH, DEC_SEQ, D), 1.0),
        'cache_diff_k': nrm(ks[2], (DEC_BATCH, DEPTH, PAST_LEN, DIFF_HEADS, 2, DIFF_DH), 1.0),
        'cache_diff_v': nrm(ks[3], (DEC_BATCH, DEPTH, PAST_LEN, DIFF_HEADS, 2 * DIFF_DH), 1.0),
        'state_gla_fwd': nrm(ks[4], (DEC_BATCH, DEPTH, GLA_HEADS, GLA_DK, GLA_DV), 0.1),
        'state_gla_bwd': nrm(ks[5], (DEC_BATCH, DEPTH, GLA_HEADS, GLA_DK, GLA_DV), 0.1),
        'c': nrm(ks[6], (DEC_BATCH, D), 1.0),
        'c_ctx': nrm(ks[7], (D,), 1.0),
        'w_mod': nrm(ks[8], (DEPTH, D, 6 * D), 0.5 * D ** -0.5),
        'b_mod': nrm(ks[9], (DEPTH, 6 * D), 0.02),
        'w_in': nrm(ks[10], (DEPTH, D, IN_WIDTH), D ** -0.5),
        'w_alpha_f': nrm(ks[11], (DEPTH, GLA_RANK, GLA_QK), GLA_RANK ** -0.5),
        'b_alpha_f': nrm(ks[12], (DEPTH, GLA_QK), 0.02),
        'w_alpha_b': nrm(ks[13], (DEPTH, GLA_RANK, GLA_QK), GLA_RANK ** -0.5),
        'b_alpha_b': nrm(ks[14], (DEPTH, GLA_QK), 0.02),
        'gla_norm_g': 1.0 + nrm(ks[15], (DEPTH, GLA_DV), 0.02),
        'w_proj_a': nrm(ks[16], (DEPTH, GLA_VW, D), GLA_VW ** -0.5),
        'lambda_q1': nrm(ks[17], (DEPTH, DIFF_DH), 0.1),
        'lambda_k1': nrm(ks[18], (DEPTH, DIFF_DH), 0.1),
        'lambda_q2': nrm(ks[19], (DEPTH, DIFF_DH), 0.1),
        'lambda_k2': nrm(ks[20], (DEPTH, DIFF_DH), 0.1),
        'subln_g': 1.0 + nrm(ks[21], (DEPTH, 2 * DIFF_DH), 0.02),
        'w_proj_b': nrm(ks[22], (DEPTH, DIFF_VW, D), DIFF_VW ** -0.5),
        'w_out': nrm(ks[23], (DEPTH, D, D), beta * D ** -0.5),
        'ln1_g': 1.0 + nrm(ks[24], (DEPTH, D), 0.02),
        'ln1_b': nrm(ks[25], (DEPTH, D), 0.02),
        'w_router': nrm(ks[26], (DEPTH, D, N_EXPERTS), D ** -0.5),
        'w_exp_gate': nrm(ks[27], (DEPTH, N_EXPERTS, D, EXPERT_FF), D ** -0.5),
        'w_exp_up': nrm(ks[28], (DEPTH, N_EXPERTS, D, EXPERT_FF), D ** -0.5),
        'w_exp_down': nrm(ks[29], (DEPTH, N_EXPERTS, EXPERT_FF, D), beta * EXPERT_FF ** -0.5),
        'ln2_g': 1.0 + nrm(ks[30], (DEPTH, D), 0.02),
        'ln2_b': nrm(ks[31], (DEPTH, D), 0.02),
    }


def reference(x_prompt, x_sample, cache_diff_k, cache_diff_v, state_gla_fwd, state_gla_bwd, c, c_ctx,
              w_mod, b_mod, w_in, w_alpha_f, b_alpha_f, w_alpha_b, b_alpha_b, gla_norm_g, w_proj_a,
              lambda_q1, lambda_k1, lambda_q2, lambda_k2, subln_g, w_proj_b, w_out, ln1_g, ln1_b,
              w_router, w_exp_gate, w_exp_up, w_exp_down, ln2_g, ln2_b):
    alpha = (2.0 * DEPTH) ** 0.25
    ang_r, ang_c = grid_angles(x_sample.shape[1])
    xp, xs = x_prompt, x_sample
    new_k, new_v, new_f, new_b = [], [], [], []
    for l in range(DEPTH):
        lam_init = 0.8 - 0.6 * math.exp(-0.3 * l)
        lam = (jnp.exp(jnp.sum(lambda_q1[l].astype(jnp.float32) * lambda_k1[l].astype(jnp.float32)))
               - jnp.exp(jnp.sum(lambda_q2[l].astype(jnp.float32) * lambda_k2[l].astype(jnp.float32)))
               + lam_init)
        mixer = functools.partial(token_mixers, w_in=w_in[l], w_alpha_f=w_alpha_f[l], b_alpha_f=b_alpha_f[l],
                                  w_alpha_b=w_alpha_b[l], b_alpha_b=b_alpha_b[l], gla_norm_g=gla_norm_g[l],
                                  w_proj_a=w_proj_a[l], lam=lam, lam_init=lam_init, subln_g=subln_g[l],
                                  w_proj_b=w_proj_b[l], w_out=w_out[l])
        moe = functools.partial(ec_moe, w_router=w_router[l], w_gate=w_exp_gate[l],
                                w_up=w_exp_up[l], w_down=w_exp_down[l])
        mod_ctx = jax.nn.silu(c_ctx)[None, :] @ w_mod[l] + b_mod[l]
        mod_lat = jax.nn.silu(c) @ w_mod[l] + b_mod[l]
        zeros = jnp.zeros((xp.shape[0], GLA_HEADS, GLA_DK, GLA_DV), jnp.float32)
        xp, kd, vd, s_f, s_b = trunk_layer(xp, mod_ctx, mixer, moe, ln1_g[l], ln1_b[l], ln2_g[l], ln2_b[l], alpha,
                                           rope_ang=None, s0_f=zeros, s0_b=zeros, k_ctx=None, v_ctx=None)
        new_k.append(kd)
        new_v.append(vd)
        new_f.append(s_f)
        new_b.append(s_b)
        xs = trunk_layer(xs, mod_lat, mixer, moe, ln1_g[l], ln1_b[l], ln2_g[l], ln2_b[l], alpha,
                         rope_ang=(ang_r, ang_c), s0_f=state_gla_fwd[:, l], s0_b=state_gla_bwd[:, l],
                         k_ctx=cache_diff_k[:, l], v_ctx=cache_diff_v[:, l])[0]
    new_diff_k = jnp.stack(new_k, axis=1)
    new_diff_v = jnp.stack(new_v, axis=1)
    new_gla_fwd = jnp.stack(new_f, axis=1)
    new_gla_bwd = jnp.stack(new_b, axis=1)
    return (xp, xs, new_diff_k, new_diff_v, new_gla_fwd, new_gla_bwd)
```

```python
import functools
import math

import jax
import jax.numpy as jnp
from jax import lax
from jax.experimental import pallas as pl
from jax.experimental.pallas import tpu as pltpu

F32 = jnp.float32
BF16 = jnp.bfloat16
HIGHEST = lax.Precision.HIGHEST

LANE = 128
EPS = 1e-6
GRID_W = 64
GLA_HEADS = 4
GLA_DK = 128
GLA_DV = 256
GLA_QK = GLA_HEADS * GLA_DK
GLA_VW = GLA_HEADS * GLA_DV
GLA_RANK = 16
GLA_TAU = 16.0
CHUNK = 64
DIFF_HEADS = 8
DIFF_DH = 64
ROPE_BASE = 10000.0
ROPE_AXIS_DIM = DIFF_DH // 2
ROPE_P = ROPE_AXIS_DIM // 2
N_EXPERTS = 16
CAPACITY_FACTOR = 2
VMEM_LIMIT = 56 * 1024 * 1024

COL_QA, COL_KA, COL_VA, COL_RA, COL_QD, COL_KD, COL_VD, COL_GA, COL_GB = (
    0, 512, 1024, 2048, 3072, 4096, 5120, 6144, 7168)
MAIN_W = 8192


def _cparams(sem):
    return pltpu.CompilerParams(dimension_semantics=sem, vmem_limit_bytes=VMEM_LIMIT)


def _ln_plain(x):
    mu = jnp.mean(x, axis=-1, keepdims=True)
    xc = x - mu
    var = jnp.mean(xc * xc, axis=-1, keepdims=True)
    return xc * lax.rsqrt(var + EPS)


def _sigmoid(x):
    return 1.0 / (1.0 + jnp.exp(-x))


def _silu(x):
    return x * _sigmoid(x)


def _mod_kernel(c_ref, w_ref, b_ref, o_ref):
    c = c_ref[...]
    o_ref[...] = jnp.dot(_silu(c), w_ref[...], precision=HIGHEST, preferred_element_type=F32) + b_ref[...]


def _modulation(c_all, w_mod, b_mod):
    rows, d = c_all.shape
    n = w_mod.shape[1]
    tn = 1024
    return pl.pallas_call(
        _mod_kernel,
        out_shape=jax.ShapeDtypeStruct((rows, n), F32),
        grid=(n // tn,),
        in_specs=[pl.BlockSpec((rows, d), lambda j: (0, 0)),
                  pl.BlockSpec((d, tn), lambda j: (0, j)),
                  pl.BlockSpec((1, tn), lambda j: (0, j))],
        out_specs=pl.BlockSpec((rows, tn), lambda j: (0, j)),
        compiler_params=_cparams(("parallel",)),
        name="modulation",
    )(c_all, w_mod, b_mod.reshape(1, n))


def _in_proj_kernel(x_ref, mod_ref, w_ref, wa_ref, y_ref, a_ref, h_scr):
    @pl.when(pl.program_id(1) == 0)
    def _():
        mod = mod_ref[0]
        h = _ln_plain(x_ref[...]) * (1.0 + mod[1:2]) + mod[0:1]
        hb = h.astype(BF16)
        h_scr[...] = hb
        a_ref[...] = jnp.dot(hb, wa_ref[...], preferred_element_type=F32)

    y_ref[...] = jnp.dot(h_scr[...], w_ref[...], preferred_element_type=F32)


def _in_proj(x, mod_all, mod_row, w_main, w_a, tm, tn):
    n, d = x.shape
    nw = w_main.shape[1]
    return pl.pallas_call(
        _in_proj_kernel,
        out_shape=(jax.ShapeDtypeStruct((n, nw), F32), jax.ShapeDtypeStruct((n, LANE), F32)),
        grid=(n // tm, nw // tn),
        in_specs=[pl.BlockSpec((tm, d), lambda i, j: (i, 0)),
                  pl.BlockSpec((1, 6, d), lambda i, j: (mod_row(i, tm), 0, 0)),
                  pl.BlockSpec((d, tn), lambda i, j: (0, j)),
                  pl.BlockSpec((d, LANE), lambda i, j: (0, 0))],
        out_specs=(pl.BlockSpec((tm, tn), lambda i, j: (i, j)),
                   pl.BlockSpec((tm, LANE), lambda i, j: (i, 0))),
        scratch_shapes=[pltpu.VMEM((tm, d), BF16)],
        compiler_params=_cparams(("parallel", "arbitrary")),
        name="in_proj",
    )(x, mod_all, w_main, w_a)


def _gla_kernel(a_ref, q_ref, k_ref, v_ref, wal_ref, bal_ref, s0_ref, o_ref, sout_ref, s_scr):
    d = pl.program_id(0)
    c = pl.program_id(2)
    nc = pl.num_programs(2)

    @pl.when(c == 0)
    def _():
        s_scr[...] = s0_ref[0, 0]

    z = jnp.dot(a_ref[...].astype(BF16), wal_ref[0], preferred_element_type=F32) + bal_ref[0]
    g = (jnp.minimum(z, 0.0) - jnp.log(1.0 + jnp.exp(-jnp.abs(z)))) * (1.0 / GLA_TAU)
    row = lax.broadcasted_iota(jnp.int32, (CHUNK, CHUNK), 0)
    col = lax.broadcasted_iota(jnp.int32, (CHUNK, CHUNK), 1)
    keep = (row - col) * (1 - 2 * d) >= 0
    tri = keep.astype(F32)
    b = jnp.dot(tri, g, precision=HIGHEST, preferred_element_type=F32)
    b_last = jnp.sum(g, axis=0, keepdims=True)
    eb = jnp.exp(b)
    enb = jnp.exp(-b)
    ekd = jnp.exp(b_last - b)
    eb_last_t = jnp.transpose(jnp.broadcast_to(jnp.exp(b_last), (8, GLA_QK)))
    q = q_ref[...] * (GLA_DK ** -0.5)
    k = k_ref[...]
    qe = (q * eb).astype(BF16)
    ke = (k * enb).astype(BF16)
    kd = (k * ekd).astype(BF16)
    v = v_ref[...].astype(BF16)
    for h in range(GLA_HEADS):
        ks = slice(h * GLA_DK, (h + 1) * GLA_DK)
        vs = slice(h * GLA_DV, (h + 1) * GLA_DV)
        s_h = s_scr[h]
        o = jnp.dot(qe[:, ks], s_h.astype(BF16), preferred_element_type=F32)
        a = lax.dot_general(qe[:, ks], ke[:, ks], (((1,), (1,)), ((), ())), preferred_element_type=F32)
        a = jnp.where(keep, a, 0.0).astype(BF16)
        o = o + jnp.dot(a, v[:, vs], preferred_element_type=F32)
        o_ref[0, :, vs] = o
        upd = lax.dot_general(kd[:, ks], v[:, vs], (((0,), (0,)), ((), ())), preferred_element_type=F32)
        s_scr[h] = eb_last_t[ks, 0:1] * s_h + upd

    @pl.when(c == nc - 1)
    def _():
        sout_ref[0, 0] = s_scr[...]


def _gla(y, a, w_alpha_pad, b_alpha, s0, batch, seq):
    n = y.shape[0]
    nc = seq // CHUNK

    def rowblk(d, b, c):
        return b * nc + c + d * (nc - 1 - 2 * c)

    return pl.pallas_call(
        _gla_kernel,
        out_shape=(jax.ShapeDtypeStruct((2, n, GLA_VW), F32),
                   jax.ShapeDtypeStruct((2, batch, GLA_HEADS, GLA_DK, GLA_DV), F32)),
        grid=(2, batch, nc),
        in_specs=[pl.BlockSpec((CHUNK, LANE), lambda d, b, c: (rowblk(d, b, c), 0)),
                  pl.BlockSpec((CHUNK, GLA_QK), lambda d, b, c: (rowblk(d, b, c), COL_QA // GLA_QK)),
                  pl.BlockSpec((CHUNK, GLA_QK), lambda d, b, c: (rowblk(d, b, c), COL_KA // GLA_QK)),
                  pl.BlockSpec((CHUNK, GLA_VW), lambda d, b, c: (rowblk(d, b, c), COL_VA // GLA_VW)),
                  pl.BlockSpec((1, LANE, GLA_QK), lambda d, b, c: (d, 0, 0)),
                  pl.BlockSpec((1, 1, GLA_QK), lambda d, b, c: (d, 0, 0)),
                  pl.BlockSpec((1, 1, GLA_HEADS, GLA_DK, GLA_DV), lambda d, b, c: (d, b, 0, 0, 0))],
        out_specs=(pl.BlockSpec((1, CHUNK, GLA_VW), lambda d, b, c: (d, rowblk(d, b, c), 0)),
                   pl.BlockSpec((1, 1, GLA_HEADS, GLA_DK, GLA_DV), lambda d, b, c: (d, b, 0, 0, 0))),
        scratch_shapes=[pltpu.VMEM((GLA_HEADS, GLA_DK, GLA_DV), F32)],
        compiler_params=_cparams(("parallel", "parallel", "arbitrary")),
        name="gla",
    )(a, y, y, y, w_alpha_pad, b_alpha, s0)


def _rope(x, cos_t, sin_a, sin_b):
    return x * cos_t + pltpu.roll(x, LANE - ROPE_P, 1) * sin_a + pltpu.roll(x, ROPE_P, 1) * sin_b


def _lambda(lam_ref, lam_init):
    lq1, lk1, lq2, lk2 = lam_ref[0:1], lam_ref[1:2], lam_ref[2:3], lam_ref[3:4]
    return (jnp.exp(jnp.sum(lq1 * lk1, axis=-1, keepdims=True))
            - jnp.exp(jnp.sum(lq2 * lk2, axis=-1, keepdims=True)) + lam_init)


def _attn_tail(s, v_parts, lam, g_ref, o_ref, tq, lam_init):
    m = jnp.max(s, axis=-1, keepdims=True)
    p = jnp.exp(s - m)
    r = 1.0 / jnp.sum(p, axis=-1, keepdims=True)
    pd = (p[:tq] * r[:tq] - lam * (p[tq:] * r[tq:])).astype(BF16)
    o = None
    off = 0
    for vp in v_parts:
        part = jnp.dot(pd[:, off:off + vp.shape[0]], vp, preferred_element_type=F32)
        o = part if o is None else o + part
        off += vp.shape[0]
    ms = jnp.mean(o * o, axis=-1, keepdims=True)
    o_ref[...] = (o * lax.rsqrt(ms + EPS) * g_ref[...] * (1.0 - lam_init)).astype(o_ref.dtype)


def _split_q(q, tq):
    lane = lax.broadcasted_iota(jnp.int32, (tq, LANE), 1)
    q = q * (DIFF_DH ** -0.5)
    return jnp.concatenate([jnp.where(lane < DIFF_DH, q, 0.0), jnp.where(lane >= DIFF_DH, q, 0.0)],
                           axis=0).astype(BF16)


def _attn_ctx_kernel(q_ref, k_ref, v_ref, lam_ref, g_ref, o_ref, *, tq, lam_init):
    lam = _lambda(lam_ref, lam_init)
    q2 = _split_q(q_ref[...], tq)
    s = lax.dot_general(q2, k_ref[...].astype(BF16), (((1,), (1,)), ((), ())), preferred_element_type=F32)
    _attn_tail(s, [v_ref[...].astype(BF16)], lam, g_ref, o_ref, tq, lam_init)


def _attn_lat_kernel(q_ref, k_ref, v_ref, kc_ref, vc_ref, cq_ref, saq_ref, sbq_ref, ck_ref, sak_ref, sbk_ref,
                     lam_ref, g_ref, o_ref, k_scr, *, tq, seq, lam_init):
    @pl.when(pl.program_id(2) == 0)
    def _():
        k_scr[0:seq] = _rope(k_ref[...], ck_ref[...], sak_ref[...], sbk_ref[...]).astype(BF16)
        k_scr[seq:] = kc_ref[...].astype(BF16)

    lam = _lambda(lam_ref, lam_init)
    q2 = _split_q(_rope(q_ref[...], cq_ref[...], saq_ref[...], sbq_ref[...]), tq)
    s = lax.dot_general(q2, k_scr[...], (((1,), (1,)), ((), ())), preferred_element_type=F32)
    _attn_tail(s, [v_ref[...].astype(BF16), vc_ref[...].astype(BF16)], lam, g_ref, o_ref, tq, lam_init)


def _diff_attn_ctx(y, lam_vecs, subln_g, batch, seq, lam_init):
    n = y.shape[0]
    tq = seq
    hb = lambda col: col // LANE
    return pl.pallas_call(
        functools.partial(_attn_ctx_kernel, tq=tq, lam_init=lam_init),
        out_shape=jax.ShapeDtypeStruct((n, DIFF_HEADS * LANE), BF16),
        grid=(batch, DIFF_HEADS),
        in_specs=[pl.BlockSpec((tq, LANE), lambda b, h: (b, hb(COL_QD) + h)),
                  pl.BlockSpec((seq, LANE), lambda b, h: (b, hb(COL_KD) + h)),
                  pl.BlockSpec((seq, LANE), lambda b, h: (b, hb(COL_VD) + h)),
                  pl.BlockSpec((4, DIFF_DH), lambda b, h: (0, 0)),
                  pl.BlockSpec((1, LANE), lambda b, h: (0, 0))],
        out_specs=pl.BlockSpec((tq, LANE), lambda b, h: (b, h)),
        compiler_params=_cparams(("parallel", "parallel")),
        name="diff_attn_ctx",
    )(y, y, y, lam_vecs, subln_g)


def _diff_attn_lat(y, k_cache, v_cache, tabs, lam_vecs, subln_g, batch, seq, past, tq, lam_init):
    n = y.shape[0]
    nq = seq // tq
    hb = lambda col: col // LANE
    cos_t, sin_a, sin_b = tabs
    qtab = pl.BlockSpec((tq, LANE), lambda b, h, i: (i, 0))
    ktab = pl.BlockSpec((seq, LANE), lambda b, h, i: (0, 0))
    return pl.pallas_call(
        functools.partial(_attn_lat_kernel, tq=tq, seq=seq, lam_init=lam_init),
        out_shape=jax.ShapeDtypeStruct((n, DIFF_HEADS * LANE), BF16),
        grid=(batch, DIFF_HEADS, nq),
        in_specs=[pl.BlockSpec((tq, LANE), lambda b, h, i: (b * nq + i, hb(COL_QD) + h)),
                  pl.BlockSpec((seq, LANE), lambda b, h, i: (b, hb(COL_KD) + h)),
                  pl.BlockSpec((seq, LANE), lambda b, h, i: (b, hb(COL_VD) + h)),
                  pl.BlockSpec((past, LANE), lambda b, h, i: (b, h)),
                  pl.BlockSpec((past, LANE), lambda b, h, i: (b, h)),
                  qtab, qtab, qtab, ktab, ktab, ktab,
                  pl.BlockSpec((4, DIFF_DH), lambda b, h, i: (0, 0)),
                  pl.BlockSpec((1, LANE), lambda b, h, i: (0, 0))],
        out_specs=pl.BlockSpec((tq, LANE), lambda b, h, i: (b * nq + i, h)),
        scratch_shapes=[pltpu.VMEM((seq + past, LANE), BF16)],
        compiler_params=_cparams(("parallel", "parallel", "arbitrary")),
        name="diff_attn_lat",
    )(y, y, y, k_cache, v_cache, cos_t, sin_a, sin_b, cos_t, sin_a, sin_b, lam_vecs, subln_g)


def _rope_tables(seq):
    rows = seq // GRID_W
    pos_r = jnp.repeat(jnp.arange(rows, dtype=F32), GRID_W)
    pos_c = jnp.tile(jnp.arange(GRID_W, dtype=F32), rows)
    inv = ROPE_BASE ** (-jnp.arange(ROPE_P, dtype=F32) / ROPE_P)
    ang_r = pos_r[:, None] * inv
    ang_c = pos_c[:, None] * inv
    zeros = jnp.zeros((seq, ROPE_P), F32)

    def axis_tabs(ang):
        cs, sn = jnp.cos(ang), jnp.sin(ang)
        return (jnp.concatenate([cs, cs], -1), jnp.concatenate([-sn, zeros], -1), jnp.concatenate([zeros, sn], -1))

    tr, tc = axis_tabs(ang_r), axis_tabs(ang_c)
    reps = LANE // (2 * ROPE_AXIS_DIM)
    return tuple(jnp.tile(jnp.concatenate([a, b], -1), (1, reps)) for a, b in zip(tr, tc))


def _merge_kernel(of_ref, ob_ref, ra_ref, od_ref, ga_ref, gb_ref, x_ref, mod_ref, gn_ref, wpa_ref, wpb_ref,
                  wout_ref, l1g_ref, l1b_ref, wr_ref, x1_ref, h2_ref, aff_ref, *, alpha):
    mod = mod_ref[0]
    o = of_ref[0] + ob_ref[0]
    parts = []
    for h in range(GLA_HEADS):
        oh = o[:, h * GLA_DV:(h + 1) * GLA_DV]
        parts.append(oh * lax.rsqrt(jnp.mean(oh * oh, axis=-1, keepdims=True) + EPS) * gn_ref[...])
    o_a = (jnp.concatenate(parts, axis=-1) * _silu(ra_ref[...])).astype(BF16)
    pa = jnp.dot(o_a, wpa_ref[...], preferred_element_type=F32)
    pb = jnp.dot(od_ref[...], wpb_ref[...], preferred_element_type=F32)
    merged = (_sigmoid(ga_ref[...]) * pa + _sigmoid(gb_ref[...]) * pb).astype(BF16)
    mix = jnp.dot(merged, wout_ref[...], preferred_element_type=F32)
    x1 = _ln_plain(alpha * x_ref[...] + mod[2:3] * mix) * l1g_ref[...] + l1b_ref[...]
    x1_ref[...] = x1
    h2 = _ln_plain(x1) * (1.0 + mod[4:5]) + mod[3:4]
    h2_ref[...] = h2.astype(h2_ref.dtype)
    logits = lax.dot_general(wr_ref[...], h2, (((1,), (1,)), ((), ())), precision=HIGHEST,
                             preferred_element_type=F32)
    e = jnp.exp(logits - jnp.max(logits, axis=0, keepdims=True))
    aff_ref[...] = e / jnp.sum(e, axis=0, keepdims=True)


def _merge(o_gla, y, o_d, x, mod_all, mod_row, gn, wpa, wpb, wout, l1g, l1b, wr_t, tm, alpha):
    n, d = x.shape
    cb = lambda col: col // d
    rowspec = lambda col: pl.BlockSpec((tm, d), lambda i: (i, col))
    const = lambda shape: pl.BlockSpec(shape, lambda i: (0,) * len(shape))
    return pl.pallas_call(
        functools.partial(_merge_kernel, alpha=alpha),
        out_shape=(jax.ShapeDtypeStruct((n, d), F32), jax.ShapeDtypeStruct((n, d), BF16),
                   jax.ShapeDtypeStruct((N_EXPERTS, n), F32)),
        grid=(n // tm,),
        in_specs=[pl.BlockSpec((1, tm, d), lambda i: (0, i, 0)),
                  pl.BlockSpec((1, tm, d), lambda i: (1, i, 0)),
                  rowspec(cb(COL_RA)), rowspec(0), rowspec(cb(COL_GA)), rowspec(cb(COL_GB)), rowspec(0),
                  pl.BlockSpec((1, 6, d), lambda i: (mod_row(i, tm), 0, 0)),
                  const((1, GLA_DV)), const((d, d)), const((d, d)), const((d, d)),
                  const((1, d)), const((1, d)), const((N_EXPERTS, d))],
        out_specs=(pl.BlockSpec((tm, d), lambda i: (i, 0)), pl.BlockSpec((tm, d), lambda i: (i, 0)),
                   pl.BlockSpec((N_EXPERTS, tm), lambda i: (0, i))),
        compiler_params=_cparams(("parallel",)),
        name="merge",
    )(o_gla, o_gla, y, o_d, y, y, x, mod_all, gn, wpa, wpb, wout, l1g, l1b, wr_t)


def _moe_kernel(xe_ref, gate_ref, wg_ref, wu_ref, wd_ref, ye_ref, *, rows):
    f = pl.program_id(1)
    nf = pl.num_programs(1)
    cap = xe_ref.shape[1]
    wg = wg_ref[0].astype(BF16)
    wu = wu_ref[0].astype(BF16)
    wd = wd_ref[0].astype(BF16)

    def body(r, carry):
        sl = pl.ds(pl.multiple_of(r * rows, rows), rows)
        xr = xe_ref[0, sl, :]
        hg = jnp.dot(xr, wg, preferred_element_type=F32)
        hu = jnp.dot(xr, wu, preferred_element_type=F32)
        hid = (_silu(hg) * hu).astype(BF16)
        contrib = jnp.dot(hid, wd, preferred_element_type=F32)

        @pl.when(f == 0)
        def _():
            ye_ref[0, sl, :] = contrib

        @pl.when(f > 0)
        def _():
            ye_ref[0, sl, :] += contrib

        @pl.when(f == nf - 1)
        def _():
            ye_ref[0, sl, :] = ye_ref[0, sl, :] * gate_ref[0, sl, :]

        return carry

    lax.fori_loop(0, cap // rows, body, 0)


def _moe(xe, gate, wg, wu, wd, ft, rows):
    e, cap, d = xe.shape
    ff = wg.shape[2]
    return pl.pallas_call(
        functools.partial(_moe_kernel, rows=rows),
        out_shape=jax.ShapeDtypeStruct((e, cap, d), F32),
        grid=(e, ff // ft),
        in_specs=[pl.BlockSpec((1, cap, d), lambda i, f: (i, 0, 0)),
                  pl.BlockSpec((1, cap, 1), lambda i, f: (i, 0, 0)),
                  pl.BlockSpec((1, d, ft), lambda i, f: (i, 0, f)),
                  pl.BlockSpec((1, d, ft), lambda i, f: (i, 0, f)),
                  pl.BlockSpec((1, ft, d), lambda i, f: (i, f, 0))],
        out_specs=pl.BlockSpec((1, cap, d), lambda i, f: (i, 0, 0)),
        compiler_params=_cparams(("parallel", "arbitrary")),
        name="moe",
    )(xe, gate, wg, wu, wd)


def _final_kernel(x1_ref, m_ref, mod_ref, g_ref, b_ref, o_ref, *, alpha):
    mod = mod_ref[0]
    o_ref[...] = _ln_plain(alpha * x1_ref[...] + mod[5:6] * m_ref[...]) * g_ref[...] + b_ref[...]


def _final(x1, moe_out, mod_all, mod_row, g, b, tm, alpha):
    n, d = x1.shape
    return pl.pallas_call(
        functools.partial(_final_kernel, alpha=alpha),
        out_shape=jax.ShapeDtypeStruct((n, d), F32),
        grid=(n // tm,),
        in_specs=[pl.BlockSpec((tm, d), lambda i: (i, 0)), pl.BlockSpec((tm, d), lambda i: (i, 0)),
                  pl.BlockSpec((1, 6, d), lambda i: (mod_row(i, tm), 0, 0)),
                  pl.BlockSpec((1, d), lambda i: (0, 0)), pl.BlockSpec((1, d), lambda i: (0, 0))],
        out_specs=pl.BlockSpec((tm, d), lambda i: (i, 0)),
        compiler_params=_cparams(("parallel",)),
        name="final",
    )(x1, moe_out, mod_all, g, b)


def _pick(n, pref):
    t = pref
    while n % t:
        t //= 2
    return t


def _trunk_group(x3, mod_all, mod_row, lw, lam_init, alpha, s0, ctx):
    batch, seq, d = x3.shape
    n = batch * seq
    x = x3.reshape(n, d)
    tm_in = _pick(seq, 1024)
    y, a = _in_proj(x, mod_all, mod_row, lw["w_main"], lw["w_a"], tm_in, 1024)
    o_gla, s_fin = _gla(y, a, lw["w_alpha_pad"], lw["b_alpha"], s0, batch, seq)
    if ctx is None:
        o_d = _diff_attn_ctx(y, lw["lam_vecs"], lw["subln_g"], batch, seq, lam_init)
    else:
        k_cache, v_cache = ctx
        past = k_cache.shape[0] // batch
        o_d = _diff_attn_lat(y, k_cache, v_cache, _rope_tables(seq), lw["lam_vecs"], lw["subln_g"],
                             batch, seq, past, _pick(seq, 256), lam_init)
    tm = _pick(seq, 256)
    x1, h2, aff_t = _merge(o_gla, y, o_d, x, mod_all, mod_row, lw["gla_norm_g"], lw["w_proj_a"], lw["w_proj_b"],
                           lw["w_out"], lw["ln1_g"], lw["ln1_b"], lw["w_router_t"], tm, alpha)
    cap = CAPACITY_FACTOR * n // N_EXPERTS
    gate, idx = lax.top_k(aff_t, cap)
    xe = jnp.take(h2, idx, axis=0)
    ye = _moe(xe, gate[..., None], lw["w_exp_gate"], lw["w_exp_up"], lw["w_exp_down"], 256, _pick(cap, 512))
    moe_out = jnp.zeros((n, d), F32).at[idx.reshape(-1)].add(ye.reshape(-1, d))
    x2 = _final(x1, moe_out, mod_all, mod_row, lw["ln2_g"], lw["ln2_b"], _pick(seq, 512), alpha)
    return x2.reshape(batch, seq, d), y, s_fin


def kernel(x_prompt, x_sample, cache_diff_k, cache_diff_v, state_gla_fwd, state_gla_bwd, c, c_ctx, w_mod, b_mod, w_in, w_alpha_f, b_alpha_f, w_alpha_b, b_alpha_b, gla_norm_g, w_proj_a, lambda_q1, lambda_k1, lambda_q2, lambda_k2, subln_g, w_proj_b, w_out, ln1_g, ln1_b, w_router, w_exp_gate, w_exp_up, w_exp_down, ln2_g, ln2_b):
    depth = w_in.shape[0]
    bp, sp, d = x_prompt.shape
    bs, ss, _ = x_sample.shape
    alpha = (2.0 * depth) ** 0.25
    xp, xs = x_prompt, x_sample

    n_mod = 1 + bs
    mod_rows = -(-n_mod // 8) * 8
    c_all = jnp.zeros((mod_rows, d), F32).at[0].set(c_ctx).at[1:n_mod].set(c)

    new_k, new_v, new_f, new_b = [], [], [], []
    for l in range(depth):
        lam_init = 0.8 - 0.6 * math.exp(-0.3 * l)
        mod_all = _modulation(c_all, w_mod[l], b_mod[l]).reshape(mod_rows, 6, d)
        wi = w_in[l]
        o = 0
        pieces = {}
        for name, width in zip(("qa", "ka", "va", "ra", "af", "ab", "qd", "kd", "vd", "ga", "gb"),
                               (GLA_QK, GLA_QK, GLA_VW, GLA_VW, GLA_RANK, GLA_RANK, d, d, d, d, d)):
            pieces[name] = wi[:, o:o + width]
            o += width
        w_main = jnp.concatenate([pieces[k] for k in ("qa", "ka", "va", "ra", "qd", "kd", "vd", "ga", "gb")],
                                 axis=1).astype(BF16)
        w_a = jnp.concatenate([pieces["af"], pieces["ab"], jnp.zeros((d, LANE - 2 * GLA_RANK), F32)],
                              axis=1).astype(BF16)
        w_alpha_pad = jnp.zeros((2, LANE, GLA_QK), F32)
        w_alpha_pad = w_alpha_pad.at[0, 0:GLA_RANK].set(w_alpha_f[l]).at[1, GLA_RANK:2 * GLA_RANK].set(w_alpha_b[l])
        lw = dict(
            w_main=w_main, w_a=w_a, w_alpha_pad=w_alpha_pad.astype(BF16),
            b_alpha=jnp.stack([b_alpha_f[l], b_alpha_b[l]])[:, None, :],
            lam_vecs=jnp.stack([lambda_q1[l], lambda_k1[l], lambda_q2[l], lambda_k2[l]]),
            subln_g=subln_g[l][None, :], gla_norm_g=gla_norm_g[l][None, :],
            w_proj_a=w_proj_a[l].astype(BF16), w_proj_b=w_proj_b[l].astype(BF16), w_out=w_out[l].astype(BF16),
            ln1_g=ln1_g[l][None, :], ln1_b=ln1_b[l][None, :], w_router_t=w_router[l].T,
            w_exp_gate=w_exp_gate[l], w_exp_up=w_exp_up[l], w_exp_down=w_exp_down[l],
            ln2_g=ln2_g[l][None, :], ln2_b=ln2_b[l][None, :])

        zeros = jnp.zeros((2, bp, GLA_HEADS, GLA_DK, GLA_DV), F32)
        xp, y_p, s_p = _trunk_group(xp, mod_all, lambda i, tm: 0, lw, lam_init, alpha, zeros, None)
        new_k.append(y_p[:, COL_KD:COL_KD + d].reshape(bp, sp, DIFF_HEADS, 2, DIFF_DH))
        new_v.append(y_p[:, COL_VD:COL_VD + d].reshape(bp, sp, DIFF_HEADS, 2 * DIFF_DH))
        new_f.append(s_p[0])
        new_b.append(s_p[1])

        s0 = jnp.stack([state_gla_fwd[:, l], state_gla_bwd[:, l]])
        past = cache_diff_k.shape[2]
        ctx = (cache_diff_k[:, l].reshape(bs * past, d), cache_diff_v[:, l].reshape(bs * past, d))
        xs, _, _ = _trunk_group(xs, mod_all, lambda i, tm: 1 + (i * tm) // ss, lw, lam_init, alpha, s0, ctx)

    return (xp, xs, jnp.stack(new_k, axis=1), jnp.stack(new_v, axis=1),
            jnp.stack(new_f, axis=1), jnp.stack(new_b, axis=1))
```

```python
import functools
import math

import jax
import jax.numpy as jnp
from jax import lax
from jax.experimental import pallas as pl
from jax.experimental.pallas import tpu as pltpu

F32 = jnp.float32
BF16 = jnp.bfloat16
HIGHEST = lax.Precision.HIGHEST

LANE = 128
EPS = 1e-6
GRID_W = 64
GLA_HEADS = 4
GLA_DK = 128
GLA_DV = 256
GLA_QK = GLA_HEADS * GLA_DK
GLA_VW = GLA_HEADS * GLA_DV
GLA_RANK = 16
GLA_TAU = 16.0
CHUNK = 64
DIFF_HEADS = 8
DIFF_DH = 64
ROPE_BASE = 10000.0
ROPE_AXIS_DIM = DIFF_DH // 2
ROPE_P = ROPE_AXIS_DIM // 2
N_EXPERTS = 16
CAPACITY_FACTOR = 2
VMEM_LIMIT = 56 * 1024 * 1024

COL_QA, COL_KA, COL_VA, COL_RA, COL_QD, COL_KD, COL_VD, COL_GA, COL_GB = (
    0, 512, 1024, 2048, 3072, 4096, 5120, 6144, 7168)
MAIN_W = 8192


def _cparams(sem):
    return pltpu.CompilerParams(dimension_semantics=sem, vmem_limit_bytes=VMEM_LIMIT)


def _ln_plain(x):
    mu = jnp.mean(x, axis=-1, keepdims=True)
    xc = x - mu
    var = jnp.mean(xc * xc, axis=-1, keepdims=True)
    return xc * lax.rsqrt(var + EPS)


def _sigmoid(x):
    return 1.0 / (1.0 + jnp.exp(-x))


def _silu(x):
    return x * _sigmoid(x)


def _mod_kernel(c_ref, w_ref, b_ref, o_ref):
    c = c_ref[...]
    o_ref[...] = jnp.dot(_silu(c), w_ref[...], precision=HIGHEST, preferred_element_type=F32) + b_ref[...]


def _modulation(c_all, w_mod, b_mod):
    rows, d = c_all.shape
    n = w_mod.shape[1]
    tn = 1024
    return pl.pallas_call(
        _mod_kernel,
        out_shape=jax.ShapeDtypeStruct((rows, n), F32),
        grid=(n // tn,),
        in_specs=[pl.BlockSpec((rows, d), lambda j: (0, 0)),
                  pl.BlockSpec((d, tn), lambda j: (0, j)),
                  pl.BlockSpec((1, tn), lambda j: (0, j))],
        out_specs=pl.BlockSpec((rows, tn), lambda j: (0, j)),
        compiler_params=_cparams(("parallel",)),
        name="modulation",
    )(c_all, w_mod, b_mod.reshape(1, n))


def _in_proj_kernel(x_ref, mod_ref, w_ref, wa_ref, y_ref, a_ref, h_scr):
    @pl.when(pl.program_id(1) == 0)
    def _():
        mod = mod_ref[0]
        h = _ln_plain(x_ref[...]) * (1.0 + mod[1:2]) + mod[0:1]
        hb = h.astype(BF16)
        h_scr[...] = hb
        a_ref[...] = jnp.dot(hb, wa_ref[...], preferred_element_type=F32)

    y_ref[...] = jnp.dot(h_scr[...], w_ref[...], preferred_element_type=F32)


def _in_proj(x, mod_all, mod_row, w_main, w_a, tm, tn):
    n, d = x.shape
    nw = w_main.shape[1]
    return pl.pallas_call(
        _in_proj_kernel,
        out_shape=(jax.ShapeDtypeStruct((n, nw), F32), jax.ShapeDtypeStruct((n, LANE), F32)),
        grid=(n // tm, nw // tn),
        in_specs=[pl.BlockSpec((tm, d), lambda i, j: (i, 0)),
                  pl.BlockSpec((1, 6, d), lambda i, j: (mod_row(i, tm), 0, 0)),
                  pl.BlockSpec((d, tn), lambda i, j: (0, j)),
                  pl.BlockSpec((d, LANE), lambda i, j: (0, 0))],
        out_specs=(pl.BlockSpec((tm, tn), lambda i, j: (i, j)),
                   pl.BlockSpec((tm, LANE), lambda i, j: (i, 0))),
        scratch_shapes=[pltpu.VMEM((tm, d), BF16)],
        compiler_params=_cparams(("parallel", "arbitrary")),
        name="in_proj",
    )(x, mod_all, w_main, w_a)


def _gla_kernel(a_ref, q_ref, k_ref, v_ref, wal_ref, bal_ref, s0_ref, o_ref, sout_ref, s_scr, *, rev, nch):
    step = pl.program_id(1)

    @pl.when(step == 0)
    def _():
        s_scr[...] = s0_ref[0]

    z = jnp.dot(a_ref[...].astype(BF16), wal_ref[...], preferred_element_type=F32) + bal_ref[...]
    g = (jnp.minimum(z, 0.0) - jnp.log(1.0 + jnp.exp(-jnp.abs(z)))) * (1.0 / GLA_TAU)
    row = lax.broadcasted_iota(jnp.int32, (CHUNK, CHUNK), 0)
    col = lax.broadcasted_iota(jnp.int32, (CHUNK, CHUNK), 1)
    keep = (col >= row) if rev else (col <= row)
    tri = keep.astype(F32)
    for ci in (reversed(range(nch)) if rev else range(nch)):
        rs = slice(ci * CHUNK, (ci + 1) * CHUNK)
        gc = g[rs]
        b = jnp.dot(tri, gc, precision=HIGHEST, preferred_element_type=F32)
        b_last = jnp.sum(gc, axis=0, keepdims=True)
        eb_last_t = jnp.transpose(jnp.broadcast_to(jnp.exp(b_last), (8, GLA_QK)))
        k = k_ref[rs, :]
        qe = (q_ref[rs, :] * (GLA_DK ** -0.5) * jnp.exp(b)).astype(BF16)
        ke = (k * jnp.exp(-b)).astype(BF16)
        kd = (k * jnp.exp(b_last - b)).astype(BF16)
        v = v_ref[rs, :].astype(BF16)
        for h in range(GLA_HEADS):
            ks = slice(h * GLA_DK, (h + 1) * GLA_DK)
            vs = slice(h * GLA_DV, (h + 1) * GLA_DV)
            s_h = s_scr[h]
            o = jnp.dot(qe[:, ks], s_h.astype(BF16), preferred_element_type=F32)
            a = lax.dot_general(qe[:, ks], ke[:, ks], (((1,), (1,)), ((), ())), preferred_element_type=F32)
            a = jnp.where(keep, a, 0.0).astype(BF16)
            o_ref[rs, vs] = o + jnp.dot(a, v[:, vs], preferred_element_type=F32)
            upd = lax.dot_general(kd[:, ks], v[:, vs], (((0,), (0,)), ((), ())), preferred_element_type=F32)
            s_scr[h] = eb_last_t[ks, 0:1] * s_h + upd

    @pl.when(step == pl.num_programs(1) - 1)
    def _():
        sout_ref[0] = s_scr[...]


def _gla(y, a, w_alpha_pad, b_alpha, s0, batch, seq, rev):
    n = y.shape[0]
    nch = min(seq // CHUNK, 8)
    rows = nch * CHUNK
    ns = seq // rows

    def rowblk(b, s):
        return b * ns + (ns - 1 - s if rev else s)

    state_spec = pl.BlockSpec((1, GLA_HEADS, GLA_DK, GLA_DV), lambda b, s: (b, 0, 0, 0))
    return pl.pallas_call(
        functools.partial(_gla_kernel, rev=rev, nch=nch),
        out_shape=(jax.ShapeDtypeStruct((n, GLA_VW), F32),
                   jax.ShapeDtypeStruct((batch, GLA_HEADS, GLA_DK, GLA_DV), F32)),
        grid=(batch, ns),
        in_specs=[pl.BlockSpec((rows, LANE), lambda b, s: (rowblk(b, s), 0)),
                  pl.BlockSpec((rows, GLA_QK), lambda b, s: (rowblk(b, s), COL_QA // GLA_QK)),
                  pl.BlockSpec((rows, GLA_QK), lambda b, s: (rowblk(b, s), COL_KA // GLA_QK)),
                  pl.BlockSpec((rows, GLA_VW), lambda b, s: (rowblk(b, s), COL_VA // GLA_VW)),
                  pl.BlockSpec((LANE, GLA_QK), lambda b, s: (0, 0)),
                  pl.BlockSpec((1, GLA_QK), lambda b, s: (0, 0)),
                  state_spec],
        out_specs=(pl.BlockSpec((rows, GLA_VW), lambda b, s: (rowblk(b, s), 0)), state_spec),
        scratch_shapes=[pltpu.VMEM((GLA_HEADS, GLA_DK, GLA_DV), F32)],
        compiler_params=_cparams(("parallel", "arbitrary")),
        name="gla_bwd" if rev else "gla_fwd",
    )(a, y, y, y, w_alpha_pad, b_alpha, s0)


def _rope(x, cos_t, sin_a, sin_b):
    return x * cos_t + pltpu.roll(x, LANE - ROPE_P, 1) * sin_a + pltpu.roll(x, ROPE_P, 1) * sin_b


def _lambda(lam_ref, lam_init):
    lq1, lk1, lq2, lk2 = lam_ref[0:1], lam_ref[1:2], lam_ref[2:3], lam_ref[3:4]
    return (jnp.exp(jnp.sum(lq1 * lk1, axis=-1, keepdims=True))
            - jnp.exp(jnp.sum(lq2 * lk2, axis=-1, keepdims=True)) + lam_init)


def _attn_tail(s, v_parts, lam, g_ref, o_ref, tq, lam_init):
    m = jnp.max(s, axis=-1, keepdims=True)
    p = jnp.exp(s - m)
    r = 1.0 / jnp.sum(p, axis=-1, keepdims=True)
    pd = (p[:tq] * r[:tq] - lam * (p[tq:] * r[tq:])).astype(BF16)
    o = None
    off = 0
    for vp in v_parts:
        part = jnp.dot(pd[:, off:off + vp.shape[0]], vp, preferred_element_type=F32)
        o = part if o is None else o + part
        off += vp.shape[0]
    ms = jnp.mean(o * o, axis=-1, keepdims=True)
    o_ref[...] = (o * lax.rsqrt(ms + EPS) * g_ref[...] * (1.0 - lam_init)).astype(o_ref.dtype)


def _split_q(q, tq):
    lane = lax.broadcasted_iota(jnp.int32, (tq, LANE), 1)
    q = q * (DIFF_DH ** -0.5)
    return jnp.concatenate([jnp.where(lane < DIFF_DH, q, 0.0), jnp.where(lane >= DIFF_DH, q, 0.0)],
                           axis=0).astype(BF16)


def _attn_ctx_kernel(q_ref, k_ref, v_ref, lam_ref, g_ref, o_ref, *, tq, lam_init):
    lam = _lambda(lam_ref, lam_init)
    q2 = _split_q(q_ref[...], tq)
    s = lax.dot_general(q2, k_ref[...].astype(BF16), (((1,), (1,)), ((), ())), preferred_element_type=F32)
    _attn_tail(s, [v_ref[...].astype(BF16)], lam, g_ref, o_ref, tq, lam_init)


def _attn_lat_kernel(q_ref, k_ref, v_ref, kc_ref, vc_ref, cq_ref, saq_ref, sbq_ref, ck_ref, sak_ref, sbk_ref,
                     lam_ref, g_ref, o_ref, k_scr, *, tq, seq, lam_init):
    @pl.when(pl.program_id(2) == 0)
    def _():
        k_scr[0:seq] = _rope(k_ref[...], ck_ref[...], sak_ref[...], sbk_ref[...]).astype(BF16)
        k_scr[seq:] = kc_ref[...].astype(BF16)

    lam = _lambda(lam_ref, lam_init)
    q2 = _split_q(_rope(q_ref[...], cq_ref[...], saq_ref[...], sbq_ref[...]), tq)
    s = lax.dot_general(q2, k_scr[...], (((1,), (1,)), ((), ())), preferred_element_type=F32)
    _attn_tail(s, [v_ref[...].astype(BF16), vc_ref[...].astype(BF16)], lam, g_ref, o_ref, tq, lam_init)


def _diff_attn_ctx(y, lam_vecs, subln_g, batch, seq, lam_init):
    n = y.shape[0]
    tq = seq
    hb = lambda col: col // LANE
    return pl.pallas_call(
        functools.partial(_attn_ctx_kernel, tq=tq, lam_init=lam_init),
        out_shape=jax.ShapeDtypeStruct((n, DIFF_HEADS * LANE), BF16),
        grid=(batch, DIFF_HEADS),
        in_specs=[pl.BlockSpec((tq, LANE), lambda b, h: (b, hb(COL_QD) + h)),
                  pl.BlockSpec((seq, LANE), lambda b, h: (b, hb(COL_KD) + h)),
                  pl.BlockSpec((seq, LANE), lambda b, h: (b, hb(COL_VD) + h)),
                  pl.BlockSpec((4, DIFF_DH), lambda b, h: (0, 0)),
                  pl.BlockSpec((1, LANE), lambda b, h: (0, 0))],
        out_specs=pl.BlockSpec((tq, LANE), lambda b, h: (b, h)),
        compiler_params=_cparams(("parallel", "parallel")),
        name="diff_attn_ctx",
    )(y, y, y, lam_vecs, subln_g)


def _diff_attn_lat(y, k_cache, v_cache, tabs, lam_vecs, subln_g, batch, seq, past, tq, lam_init):
    n = y.shape[0]
    nq = seq // tq
    hb = lambda col: col // LANE
    cos_t, sin_a, sin_b = tabs
    qtab = pl.BlockSpec((tq, LANE), lambda b, h, i: (i, 0))
    ktab = pl.BlockSpec((seq, LANE), lambda b, h, i: (0, 0))
    return pl.pallas_call(
        functools.partial(_attn_lat_kernel, tq=tq, seq=seq, lam_init=lam_init),
        out_shape=jax.ShapeDtypeStruct((n, DIFF_HEADS * LANE), BF16),
        grid=(batch, DIFF_HEADS, nq),
        in_specs=[pl.BlockSpec((tq, LANE), lambda b, h, i: (b * nq + i, hb(COL_QD) + h)),
                  pl.BlockSpec((seq, LANE), lambda b, h, i: (b, hb(COL_KD) + h)),
                  pl.BlockSpec((seq, LANE), lambda b, h, i: (b, hb(COL_VD) + h)),
                  pl.BlockSpec((past, LANE), lambda b, h, i: (b, h)),
                  pl.BlockSpec((past, LANE), lambda b, h, i: (b, h)),
                  qtab, qtab, qtab, ktab, ktab, ktab,
                  pl.BlockSpec((4, DIFF_DH), lambda b, h, i: (0, 0)),
                  pl.BlockSpec((1, LANE), lambda b, h, i: (0, 0))],
        out_specs=pl.BlockSpec((tq, LANE), lambda b, h, i: (b * nq + i, h)),
        scratch_shapes=[pltpu.VMEM((seq + past, LANE), BF16)],
        compiler_params=_cparams(("parallel", "parallel", "arbitrary")),
        name="diff_attn_lat",
    )(y, y, y, k_cache, v_cache, cos_t, sin_a, sin_b, cos_t, sin_a, sin_b, lam_vecs, subln_g)


def _rope_tables(seq):
    rows = seq // GRID_W
    pos_r = jnp.repeat(jnp.arange(rows, dtype=F32), GRID_W)
    pos_c = jnp.tile(jnp.arange(GRID_W, dtype=F32), rows)
    inv = ROPE_BASE ** (-jnp.arange(ROPE_P, dtype=F32) / ROPE_P)
    ang_r = pos_r[:, None] * inv
    ang_c = pos_c[:, None] * inv
    zeros = jnp.zeros((seq, ROPE_P), F32)

    def axis_tabs(ang):
        cs, sn = jnp.cos(ang), jnp.sin(ang)
        return (jnp.concatenate([cs, cs], -1), jnp.concatenate([-sn, zeros], -1), jnp.concatenate([zeros, sn], -1))

    tr, tc = axis_tabs(ang_r), axis_tabs(ang_c)
    reps = LANE // (2 * ROPE_AXIS_DIM)
    return tuple(jnp.tile(jnp.concatenate([a, b], -1), (1, reps)) for a, b in zip(tr, tc))


def _merge_kernel(of_ref, ob_ref, ra_ref, od_ref, ga_ref, gb_ref, x_ref, mod_ref, gn_ref, wpa_ref, wpb_ref,
                  wout_ref, l1g_ref, l1b_ref, wr_ref, x1_ref, h2_ref, aff_ref, *, alpha):
    mod = mod_ref[0]
    o = of_ref[...] + ob_ref[...]
    parts = []
    for h in range(GLA_HEADS):
        oh = o[:, h * GLA_DV:(h + 1) * GLA_DV]
        parts.append(oh * lax.rsqrt(jnp.mean(oh * oh, axis=-1, keepdims=True) + EPS) * gn_ref[...])
    o_a = (jnp.concatenate(parts, axis=-1) * _silu(ra_ref[...])).astype(BF16)
    pa = jnp.dot(o_a, wpa_ref[...], preferred_element_type=F32)
    pb = jnp.dot(od_ref[...], wpb_ref[...], preferred_element_type=F32)
    merged = (_sigmoid(ga_ref[...]) * pa + _sigmoid(gb_ref[...]) * pb).astype(BF16)
    mix = jnp.dot(merged, wout_ref[...], preferred_element_type=F32)
    x1 = _ln_plain(alpha * x_ref[...] + mod[2:3] * mix) * l1g_ref[...] + l1b_ref[...]
    x1_ref[...] = x1
    h2 = _ln_plain(x1) * (1.0 + mod[4:5]) + mod[3:4]
    h2_ref[...] = h2.astype(h2_ref.dtype)
    logits = lax.dot_general(wr_ref[...], h2, (((1,), (1,)), ((), ())), precision=HIGHEST,
                             preferred_element_type=F32)
    e = jnp.exp(logits - jnp.max(logits, axis=0, keepdims=True))
    aff_ref[...] = e / jnp.sum(e, axis=0, keepdims=True)


def _merge(o_f, o_b, y, o_d, x, mod_all, mod_row, gn, wpa, wpb, wout, l1g, l1b, wr_t, tm, alpha):
    n, d = x.shape
    cb = lambda col: col // d
    rowspec = lambda col: pl.BlockSpec((tm, d), lambda i: (i, col))
    const = lambda shape: pl.BlockSpec(shape, lambda i: (0,) * len(shape))
    return pl.pallas_call(
        functools.partial(_merge_kernel, alpha=alpha),
        out_shape=(jax.ShapeDtypeStruct((n, d), F32), jax.ShapeDtypeStruct((n, d), F32),
                   jax.ShapeDtypeStruct((N_EXPERTS, n), F32)),
        grid=(n // tm,),
        in_specs=[rowspec(0), rowspec(0),
                  rowspec(cb(COL_RA)), rowspec(0), rowspec(cb(COL_GA)), rowspec(cb(COL_GB)), rowspec(0),
                  pl.BlockSpec((1, 6, d), lambda i: (mod_row(i, tm), 0, 0)),
                  const((1, GLA_DV)), const((d, d)), const((d, d)), const((d, d)),
                  const((1, d)), const((1, d)), const((N_EXPERTS, d))],
        out_specs=(pl.BlockSpec((tm, d), lambda i: (i, 0)), pl.BlockSpec((tm, d), lambda i: (i, 0)),
                   pl.BlockSpec((N_EXPERTS, tm), lambda i: (0, i))),
        compiler_params=_cparams(("parallel",)),
        name="merge",
    )(o_f, o_b, y, o_d, y, y, x, mod_all, gn, wpa, wpb, wout, l1g, l1b, wr_t)


def _moe_kernel(idx_ref, h_hbm, gate_ref, wg_ref, wu_ref, wd_ref, ye_ref, xf_scr, xb_scr, sem, *, rows):
    e = pl.program_id(0)
    f = pl.program_id(1)
    nf = pl.num_programs(1)
    cap = xf_scr.shape[0]

    @pl.when(f == 0)
    def _():
        def issue(r, carry):
            tok = idx_ref[e * cap + r]
            pltpu.make_async_copy(h_hbm.at[pl.ds(tok, 1)], xf_scr.at[pl.ds(r, 1)], sem).start()
            return carry

        lax.fori_loop(0, cap, issue, 0, unroll=8)

        def drain(r, carry):
            pltpu.make_async_copy(h_hbm.at[pl.ds(0, 1)], xf_scr.at[pl.ds(r, 1)], sem).wait()
            return carry

        lax.fori_loop(0, cap, drain, 0, unroll=8)
        xb_scr[...] = xf_scr[...].astype(BF16)

    wg = wg_ref[0].astype(BF16)
    wu = wu_ref[0].astype(BF16)
    wd = wd_ref[0].astype(BF16)

    def body(r, carry):
        sl = pl.ds(pl.multiple_of(r * rows, rows), rows)
        xr = xb_scr[sl, :]
        hg = jnp.dot(xr, wg, preferred_element_type=F32)
        hu = jnp.dot(xr, wu, preferred_element_type=F32)
        hid = (_silu(hg) * hu).astype(BF16)
        contrib = jnp.dot(hid, wd, preferred_element_type=F32)

        @pl.when(f == 0)
        def _():
            ye_ref[0, sl, :] = contrib

        @pl.when(f > 0)
        def _():
            ye_ref[0, sl, :] += contrib

        @pl.when(f == nf - 1)
        def _():
            ye_ref[0, sl, :] = ye_ref[0, sl, :] * gate_ref[0, sl, :]

        return carry

    lax.fori_loop(0, cap // rows, body, 0)


def _moe(idx, h, gate, wg, wu, wd, ft, rows):
    e, cap = idx.shape
    d = h.shape[1]
    ff = wg.shape[2]
    grid_spec = pltpu.PrefetchScalarGridSpec(
        num_scalar_prefetch=1,
        grid=(e, ff // ft),
        in_specs=[pl.BlockSpec(memory_space=pl.ANY),
                  pl.BlockSpec((1, cap, 1), lambda i, f, idx_ref: (i, 0, 0)),
                  pl.BlockSpec((1, d, ft), lambda i, f, idx_ref: (i, 0, f)),
                  pl.BlockSpec((1, d, ft), lambda i, f, idx_ref: (i, 0, f)),
                  pl.BlockSpec((1, ft, d), lambda i, f, idx_ref: (i, f, 0))],
        out_specs=pl.BlockSpec((1, cap, d), lambda i, f, idx_ref: (i, 0, 0)),
        scratch_shapes=[pltpu.VMEM((cap, d), F32), pltpu.VMEM((cap, d), BF16), pltpu.SemaphoreType.DMA])
    return pl.pallas_call(
        functools.partial(_moe_kernel, rows=rows),
        out_shape=jax.ShapeDtypeStruct((e, cap, d), F32),
        grid_spec=grid_spec,
        compiler_params=_cparams(("arbitrary", "arbitrary")),
        name="moe",
    )(idx.reshape(-1), h, gate, wg, wu, wd)


def _final_kernel(x1_ref, m_ref, mod_ref, g_ref, b_ref, o_ref, *, alpha):
    mod = mod_ref[0]
    o_ref[...] = _ln_plain(alpha * x1_ref[...] + mod[5:6] * m_ref[...]) * g_ref[...] + b_ref[...]


def _final(x1, moe_out, mod_all, mod_row, g, b, tm, alpha):
    n, d = x1.shape
    return pl.pallas_call(
        functools.partial(_final_kernel, alpha=alpha),
        out_shape=jax.ShapeDtypeStruct((n, d), F32),
        grid=(n // tm,),
        in_specs=[pl.BlockSpec((tm, d), lambda i: (i, 0)), pl.BlockSpec((tm, d), lambda i: (i, 0)),
                  pl.BlockSpec((1, 6, d), lambda i: (mod_row(i, tm), 0, 0)),
                  pl.BlockSpec((1, d), lambda i: (0, 0)), pl.BlockSpec((1, d), lambda i: (0, 0))],
        out_specs=pl.BlockSpec((tm, d), lambda i: (i, 0)),
        compiler_params=_cparams(("parallel",)),
        name="final",
    )(x1, moe_out, mod_all, g, b)


def _pick(n, pref):
    t = pref
    while n % t:
        t //= 2
    return t


def _trunk_group(x3, mod_all, mod_row, lw, lam_init, alpha, s0, ctx):
    batch, seq, d = x3.shape
    n = batch * seq
    x = x3.reshape(n, d)
    tile_span = n if ctx is None else seq
    y, a = _in_proj(x, mod_all, mod_row, lw["w_main"], lw["w_a"], _pick(tile_span, 1024), 1024)
    o_f, s_f = _gla(y, a, lw["w_alpha_pad"][0], lw["b_alpha"][0], s0[0], batch, seq, False)
    o_b, s_b = _gla(y, a, lw["w_alpha_pad"][1], lw["b_alpha"][1], s0[1], batch, seq, True)
    if ctx is None:
        o_d = _diff_attn_ctx(y, lw["lam_vecs"], lw["subln_g"], batch, seq, lam_init)
    else:
        k_cache, v_cache = ctx
        past = k_cache.shape[0] // batch
        o_d = _diff_attn_lat(y, k_cache, v_cache, _rope_tables(seq), lw["lam_vecs"], lw["subln_g"],
                             batch, seq, past, _pick(seq, 256), lam_init)
    tm = _pick(seq, 256)
    x1, h2, aff_t = _merge(o_f, o_b, y, o_d, x, mod_all, mod_row, lw["gla_norm_g"], lw["w_proj_a"], lw["w_proj_b"],
                           lw["w_out"], lw["ln1_g"], lw["ln1_b"], lw["w_router_t"], tm, alpha)
    cap = CAPACITY_FACTOR * n // N_EXPERTS
    gate, idx = lax.top_k(aff_t, cap)
    ye = _moe(idx, h2, gate[..., None], lw["w_exp_gate"], lw["w_exp_up"], lw["w_exp_down"], 256, _pick(cap, 512))
    moe_out = jnp.zeros((n, d), F32).at[idx.reshape(-1)].add(ye.reshape(-1, d))
    x2 = _final(x1, moe_out, mod_all, mod_row, lw["ln2_g"], lw["ln2_b"], _pick(tile_span, 512), alpha)
    return x2.reshape(batch, seq, d), y, (s_f, s_b)


def kernel(x_prompt, x_sample, cache_diff_k, cache_diff_v, state_gla_fwd, state_gla_bwd, c, c_ctx, w_mod, b_mod, w_in, w_alpha_f, b_alpha_f, w_alpha_b, b_alpha_b, gla_norm_g, w_proj_a, lambda_q1, lambda_k1, lambda_q2, lambda_k2, subln_g, w_proj_b, w_out, ln1_g, ln1_b, w_router, w_exp_gate, w_exp_up, w_exp_down, ln2_g, ln2_b):
    depth = w_in.shape[0]
    bp, sp, d = x_prompt.shape
    bs, ss, _ = x_sample.shape
    alpha = (2.0 * depth) ** 0.25
    xp, xs = x_prompt, x_sample

    n_mod = 1 + bs
    mod_rows = -(-n_mod // 8) * 8
    c_all = jnp.zeros((mod_rows, d), F32).at[0].set(c_ctx).at[1:n_mod].set(c)

    new_k, new_v, new_f, new_b = [], [], [], []
    for l in range(depth):
        lam_init = 0.8 - 0.6 * math.exp(-0.3 * l)
        mod_all = _modulation(c_all, w_mod[l], b_mod[l]).reshape(mod_rows, 6, d)
        wi = w_in[l]
        o = 0
        pieces = {}
        for name, width in zip(("qa", "ka", "va", "ra", "af", "ab", "qd", "kd", "vd", "ga", "gb"),
                               (GLA_QK, GLA_QK, GLA_VW, GLA_VW, GLA_RANK, GLA_RANK, d, d, d, d, d)):
            pieces[name] = wi[:, o:o + width]
            o += width
        w_main = jnp.concatenate([pieces[k] for k in ("qa", "ka", "va", "ra", "qd", "kd", "vd", "ga", "gb")],
                                 axis=1).astype(BF16)
        w_a = jnp.concatenate([pieces["af"], pieces["ab"], jnp.zeros((d, LANE - 2 * GLA_RANK), F32)],
                              axis=1).astype(BF16)
        w_alpha_pad = jnp.zeros((2, LANE, GLA_QK), F32)
        w_alpha_pad = w_alpha_pad.at[0, 0:GLA_RANK].set(w_alpha_f[l]).at[1, GLA_RANK:2 * GLA_RANK].set(w_alpha_b[l])
        lw = dict(
            w_main=w_main, w_a=w_a, w_alpha_pad=w_alpha_pad.astype(BF16),
            b_alpha=jnp.stack([b_alpha_f[l], b_alpha_b[l]])[:, None, :],
            lam_vecs=jnp.stack([lambda_q1[l], lambda_k1[l], lambda_q2[l], lambda_k2[l]]),
            subln_g=subln_g[l][None, :], gla_norm_g=gla_norm_g[l][None, :],
            w_proj_a=w_proj_a[l].astype(BF16), w_proj_b=w_proj_b[l].astype(BF16), w_out=w_out[l].astype(BF16),
            ln1_g=ln1_g[l][None, :], ln1_b=ln1_b[l][None, :], w_router_t=w_router[l].T,
            w_exp_gate=w_exp_gate[l], w_exp_up=w_exp_up[l], w_exp_down=w_exp_down[l],
            ln2_g=ln2_g[l][None, :], ln2_b=ln2_b[l][None, :])

        zeros = jnp.zeros((2, bp, GLA_HEADS, GLA_DK, GLA_DV), F32)
        xp, y_p, s_p = _trunk_group(xp, mod_all, lambda i, tm: 0, lw, lam_init, alpha, zeros, None)
        new_k.append(y_p[:, COL_KD:COL_KD + d].reshape(bp, sp, DIFF_HEADS, 2, DIFF_DH))
        new_v.append(y_p[:, COL_VD:COL_VD + d].reshape(bp, sp, DIFF_HEADS, 2 * DIFF_DH))
        new_f.append(s_p[0])
        new_b.append(s_p[1])

        s0 = jnp.stack([state_gla_fwd[:, l], state_gla_bwd[:, l]])
        past = cache_diff_k.shape[2]
        ctx = (cache_diff_k[:, l].reshape(bs * past, d), cache_diff_v[:, l].reshape(bs * past, d))
        xs, _, _ = _trunk_group(xs, mod_all, lambda i, tm: 1 + (i * tm) // ss, lw, lam_init, alpha, s0, ctx)

    return (xp, xs, jnp.stack(new_k, axis=1), jnp.stack(new_v, axis=1),
            jnp.stack(new_f, axis=1), jnp.stack(new_b, axis=1))
```

```python
import functools
import math

import jax
import jax.numpy as jnp
from jax import lax
from jax.experimental import pallas as pl
from jax.experimental.pallas import tpu as pltpu

F32 = jnp.float32
BF16 = jnp.bfloat16
HIGHEST = lax.Precision.HIGHEST

LANE = 128
EPS = 1e-6
GRID_W = 64
GLA_HEADS = 4
GLA_DK = 128
GLA_DV = 256
GLA_QK = GLA_HEADS * GLA_DK
GLA_VW = GLA_HEADS * GLA_DV
GLA_RANK = 16
GLA_TAU = 16.0
CHUNK = 64
DIFF_HEADS = 8
DIFF_DH = 64
ROPE_BASE = 10000.0
ROPE_AXIS_DIM = DIFF_DH // 2
ROPE_P = ROPE_AXIS_DIM // 2
N_EXPERTS = 16
CAPACITY_FACTOR = 2
VMEM_LIMIT = 56 * 1024 * 1024

COL_QA, COL_KA, COL_VA, COL_RA, COL_QD, COL_KD, COL_VD, COL_GA, COL_GB = (
    0, 512, 1024, 2048, 3072, 4096, 5120, 6144, 7168)
MAIN_W = 8192


def _cparams(sem):
    return pltpu.CompilerParams(dimension_semantics=sem, vmem_limit_bytes=VMEM_LIMIT)


def _ln_plain(x):
    mu = jnp.mean(x, axis=-1, keepdims=True)
    xc = x - mu
    var = jnp.mean(xc * xc, axis=-1, keepdims=True)
    return xc * lax.rsqrt(var + EPS)


def _sigmoid(x):
    return 1.0 / (1.0 + jnp.exp(-x))


def _silu(x):
    return x * _sigmoid(x)


def _mod_kernel(c_ref, w_ref, b_ref, o_ref):
    c = c_ref[...]
    o_ref[...] = jnp.dot(_silu(c), w_ref[...], precision=HIGHEST, preferred_element_type=F32) + b_ref[...]


def _modulation(c_all, w_mod, b_mod):
    rows, d = c_all.shape
    n = w_mod.shape[1]
    tn = 1024
    return pl.pallas_call(
        _mod_kernel,
        out_shape=jax.ShapeDtypeStruct((rows, n), F32),
        grid=(n // tn,),
        in_specs=[pl.BlockSpec((rows, d), lambda j: (0, 0)),
                  pl.BlockSpec((d, tn), lambda j: (0, j)),
                  pl.BlockSpec((1, tn), lambda j: (0, j))],
        out_specs=pl.BlockSpec((rows, tn), lambda j: (0, j)),
        compiler_params=_cparams(("parallel",)),
        name="modulation",
    )(c_all, w_mod, b_mod.reshape(1, n))


def _in_proj_kernel(x_ref, mod_ref, w_ref, wa_ref, y_ref, a_ref, h_scr):
    @pl.when(pl.program_id(1) == 0)
    def _():
        mod = mod_ref[0]
        h = _ln_plain(x_ref[...]) * (1.0 + mod[1:2]) + mod[0:1]
        hb = h.astype(BF16)
        h_scr[...] = hb
        a_ref[...] = jnp.dot(hb, wa_ref[...], preferred_element_type=F32)

    y_ref[...] = jnp.dot(h_scr[...], w_ref[...], preferred_element_type=F32)


def _in_proj(x, mod_all, mod_row, w_main, w_a, tm, tn):
    n, d = x.shape
    nw = w_main.shape[1]
    return pl.pallas_call(
        _in_proj_kernel,
        out_shape=(jax.ShapeDtypeStruct((n, nw), F32), jax.ShapeDtypeStruct((n, LANE), F32)),
        grid=(n // tm, nw // tn),
        in_specs=[pl.BlockSpec((tm, d), lambda i, j: (i, 0)),
                  pl.BlockSpec((1, 6, d), lambda i, j: (mod_row(i, tm), 0, 0)),
                  pl.BlockSpec((d, tn), lambda i, j: (0, j)),
                  pl.BlockSpec((d, LANE), lambda i, j: (0, 0))],
        out_specs=(pl.BlockSpec((tm, tn), lambda i, j: (i, j)),
                   pl.BlockSpec((tm, LANE), lambda i, j: (i, 0))),
        scratch_shapes=[pltpu.VMEM((tm, d), BF16)],
        compiler_params=_cparams(("parallel", "arbitrary")),
        name="in_proj",
    )(x, mod_all, w_main, w_a)


def _gla_kernel(a_ref, q_ref, k_ref, v_ref, wal_ref, bal_ref, s0_ref, o_ref, sout_ref, s_scr, *, rev, nch):
    step = pl.program_id(1)

    @pl.when(step == 0)
    def _():
        s_scr[...] = s0_ref[0]

    z = jnp.dot(a_ref[...].astype(BF16), wal_ref[...], preferred_element_type=F32) + bal_ref[...]
    g = (jnp.minimum(z, 0.0) - jnp.log(1.0 + jnp.exp(-jnp.abs(z)))) * (1.0 / GLA_TAU)
    row = lax.broadcasted_iota(jnp.int32, (CHUNK, CHUNK), 0)
    col = lax.broadcasted_iota(jnp.int32, (CHUNK, CHUNK), 1)
    keep = (col >= row) if rev else (col <= row)
    tri = keep.astype(F32)
    chunks = list(reversed(range(nch)) if rev else range(nch))
    heads = [(slice(h * GLA_DK, (h + 1) * GLA_DK), slice(h * GLA_DV, (h + 1) * GLA_DV)) for h in range(GLA_HEADS)]
    pre = {}
    for ci in chunks:
        rs = slice(ci * CHUNK, (ci + 1) * CHUNK)
        gc = g[rs]
        b = jnp.dot(tri, gc, precision=HIGHEST, preferred_element_type=F32)
        b_last = jnp.sum(gc, axis=0, keepdims=True)
        eb_last_t = jnp.transpose(jnp.broadcast_to(jnp.exp(b_last), (8, GLA_QK)))
        k = k_ref[rs, :]
        pre[ci] = dict(rs=rs, eb_last_t=eb_last_t,
                       qe=(q_ref[rs, :] * (GLA_DK ** -0.5) * jnp.exp(b)).astype(BF16),
                       ke=(k * jnp.exp(-b)).astype(BF16),
                       kd=(k * jnp.exp(b_last - b)).astype(BF16),
                       v=v_ref[rs, :].astype(BF16))
    for ci in chunks:
        c = pre[ci]
        c["a"] = [jnp.where(keep, lax.dot_general(c["qe"][:, ks], c["ke"][:, ks], (((1,), (1,)), ((), ())),
                                                  preferred_element_type=F32), 0.0).astype(BF16)
                  for ks, _ in heads]
    for ci in chunks:
        c = pre[ci]
        c["o"] = [jnp.dot(c["a"][h], c["v"][:, vs], preferred_element_type=F32) for h, (_, vs) in enumerate(heads)]
        c["upd"] = [lax.dot_general(c["kd"][:, ks], c["v"][:, vs], (((0,), (0,)), ((), ())),
                                    preferred_element_type=F32) for ks, vs in heads]
    state = [s_scr[h] for h in range(GLA_HEADS)]
    for ci in chunks:
        c = pre[ci]
        for h, (ks, vs) in enumerate(heads):
            o_ref[c["rs"], vs] = c["o"][h] + jnp.dot(c["qe"][:, ks], state[h].astype(BF16),
                                                     preferred_element_type=F32)
            state[h] = c["eb_last_t"][ks, 0:1] * state[h] + c["upd"][h]
    for h in range(GLA_HEADS):
        s_scr[h] = state[h]

    @pl.when(step == pl.num_programs(1) - 1)
    def _():
        sout_ref[0] = s_scr[...]


def _gla(y, a, w_alpha_pad, b_alpha, s0, batch, seq, rev):
    n = y.shape[0]
    nch = min(seq // CHUNK, 8)
    rows = nch * CHUNK
    ns = seq // rows

    def rowblk(b, s):
        return b * ns + (ns - 1 - s if rev else s)

    state_spec = pl.BlockSpec((1, GLA_HEADS, GLA_DK, GLA_DV), lambda b, s: (b, 0, 0, 0))
    return pl.pallas_call(
        functools.partial(_gla_kernel, rev=rev, nch=nch),
        out_shape=(jax.ShapeDtypeStruct((n, GLA_VW), F32),
                   jax.ShapeDtypeStruct((batch, GLA_HEADS, GLA_DK, GLA_DV), F32)),
        grid=(batch, ns),
        in_specs=[pl.BlockSpec((rows, LANE), lambda b, s: (rowblk(b, s), 0)),
                  pl.BlockSpec((rows, GLA_QK), lambda b, s: (rowblk(b, s), COL_QA // GLA_QK)),
                  pl.BlockSpec((rows, GLA_QK), lambda b, s: (rowblk(b, s), COL_KA // GLA_QK)),
                  pl.BlockSpec((rows, GLA_VW), lambda b, s: (rowblk(b, s), COL_VA // GLA_VW)),
                  pl.BlockSpec((LANE, GLA_QK), lambda b, s: (0, 0)),
                  pl.BlockSpec((1, GLA_QK), lambda b, s: (0, 0)),
                  state_spec],
        out_specs=(pl.BlockSpec((rows, GLA_VW), lambda b, s: (rowblk(b, s), 0)), state_spec),
        scratch_shapes=[pltpu.VMEM((GLA_HEADS, GLA_DK, GLA_DV), F32)],
        compiler_params=_cparams(("parallel", "arbitrary")),
        name="gla_bwd" if rev else "gla_fwd",
    )(a, y, y, y, w_alpha_pad, b_alpha, s0)


def _rope(x, cos_t, sin_a, sin_b):
    return x * cos_t + pltpu.roll(x, LANE - ROPE_P, 1) * sin_a + pltpu.roll(x, ROPE_P, 1) * sin_b


def _lambda(lam_ref, lam_init):
    lq1, lk1, lq2, lk2 = lam_ref[0:1], lam_ref[1:2], lam_ref[2:3], lam_ref[3:4]
    return (jnp.exp(jnp.sum(lq1 * lk1, axis=-1, keepdims=True))
            - jnp.exp(jnp.sum(lq2 * lk2, axis=-1, keepdims=True)) + lam_init)


ATTN_SUB = 128


def _attn_core(q, k, v, lam, g_ref, o_ref, lam_init):
    tq = q.shape[0]
    sub = min(tq, ATTN_SUB)
    lane = lax.broadcasted_iota(jnp.int32, (sub, LANE), 1)
    blocks = [slice(i * sub, (i + 1) * sub) for i in range(tq // sub)]
    s2s = []
    for rows in blocks:
        qs = q[rows] * (DIFF_DH ** -0.5 * math.log2(math.e))
        q2 = jnp.concatenate([jnp.where(lane < DIFF_DH, qs, 0.0), jnp.where(lane >= DIFF_DH, qs, 0.0)],
                             axis=0).astype(BF16)
        s2s.append(lax.dot_general(q2, k, (((1,), (1,)), ((), ())), preferred_element_type=F32))
    ps = [jnp.exp2(s2 - jnp.max(s2, axis=-1, keepdims=True)) for s2 in s2s]
    rs = [1.0 / jnp.sum(p, axis=-1, keepdims=True) for p in ps]
    pvs = [jnp.dot(p.astype(BF16), v, preferred_element_type=F32) for p in ps]
    for rows, pv, r in zip(blocks, pvs, rs):
        pv = pv * r
        o = pv[:sub] - lam * pv[sub:]
        ms = jnp.mean(o * o, axis=-1, keepdims=True)
        o_ref[rows, :] = (o * lax.rsqrt(ms + EPS) * g_ref[...] * (1.0 - lam_init)).astype(o_ref.dtype)


def _attn_ctx_kernel(q_ref, k_ref, v_ref, lam_ref, g_ref, o_ref, *, lam_init):
    lam = _lambda(lam_ref, lam_init)
    _attn_core(q_ref[...], k_ref[...].astype(BF16), v_ref[...].astype(BF16), lam, g_ref, o_ref, lam_init)


def _attn_lat_kernel(q_ref, k_ref, v_ref, kc_ref, vc_ref, cq_ref, saq_ref, sbq_ref, ck_ref, sak_ref, sbk_ref,
                     lam_ref, g_ref, o_ref, k_scr, v_scr, *, seq, lam_init):
    @pl.when(pl.program_id(2) == 0)
    def _():
        k_scr[0:seq] = _rope(k_ref[...], ck_ref[...], sak_ref[...], sbk_ref[...]).astype(BF16)
        k_scr[seq:] = kc_ref[...].astype(BF16)
        v_scr[0:seq] = v_ref[...].astype(BF16)
        v_scr[seq:] = vc_ref[...].astype(BF16)

    lam = _lambda(lam_ref, lam_init)
    q = _rope(q_ref[...], cq_ref[...], saq_ref[...], sbq_ref[...])
    _attn_core(q, k_scr[...], v_scr[...], lam, g_ref, o_ref, lam_init)


def _diff_attn_ctx(y, lam_vecs, subln_g, batch, seq, lam_init):
    n = y.shape[0]
    tq = seq
    hb = lambda col: col // LANE
    return pl.pallas_call(
        functools.partial(_attn_ctx_kernel, lam_init=lam_init),
        out_shape=jax.ShapeDtypeStruct((n, DIFF_HEADS * LANE), BF16),
        grid=(batch, DIFF_HEADS),
        in_specs=[pl.BlockSpec((tq, LANE), lambda b, h: (b, hb(COL_QD) + h)),
                  pl.BlockSpec((seq, LANE), lambda b, h: (b, hb(COL_KD) + h)),
                  pl.BlockSpec((seq, LANE), lambda b, h: (b, hb(COL_VD) + h)),
                  pl.BlockSpec((4, DIFF_DH), lambda b, h: (0, 0)),
                  pl.BlockSpec((1, LANE), lambda b, h: (0, 0))],
        out_specs=pl.BlockSpec((tq, LANE), lambda b, h: (b, h)),
        compiler_params=_cparams(("parallel", "parallel")),
        name="diff_attn_ctx",
    )(y, y, y, lam_vecs, subln_g)


def _diff_attn_lat(y, k_cache, v_cache, tabs, lam_vecs, subln_g, batch, seq, past, tq, lam_init):
    n = y.shape[0]
    nq = seq // tq
    hb = lambda col: col // LANE
    cos_t, sin_a, sin_b = tabs
    qtab = pl.BlockSpec((tq, LANE), lambda b, h, i: (i, 0))
    ktab = pl.BlockSpec((seq, LANE), lambda b, h, i: (0, 0))
    return pl.pallas_call(
        functools.partial(_attn_lat_kernel, seq=seq, lam_init=lam_init),
        out_shape=jax.ShapeDtypeStruct((n, DIFF_HEADS * LANE), BF16),
        grid=(batch, DIFF_HEADS, nq),
        in_specs=[pl.BlockSpec((tq, LANE), lambda b, h, i: (b * nq + i, hb(COL_QD) + h)),
                  pl.BlockSpec((seq, LANE), lambda b, h, i: (b, hb(COL_KD) + h)),
                  pl.BlockSpec((seq, LANE), lambda b, h, i: (b, hb(COL_VD) + h)),
                  pl.BlockSpec((past, LANE), lambda b, h, i: (b, h)),
                  pl.BlockSpec((past, LANE), lambda b, h, i: (b, h)),
                  qtab, qtab, qtab, ktab, ktab, ktab,
                  pl.BlockSpec((4, DIFF_DH), lambda b, h, i: (0, 0)),
                  pl.BlockSpec((1, LANE), lambda b, h, i: (0, 0))],
        out_specs=pl.BlockSpec((tq, LANE), lambda b, h, i: (b * nq + i, h)),
        scratch_shapes=[pltpu.VMEM((seq + past, LANE), BF16), pltpu.VMEM((seq + past, LANE), BF16)],
        compiler_params=_cparams(("parallel", "parallel", "arbitrary")),
        name="diff_attn_lat",
    )(y, y, y, k_cache, v_cache, cos_t, sin_a, sin_b, cos_t, sin_a, sin_b, lam_vecs, subln_g)


def _rope_tables(seq):
    rows = seq // GRID_W
    pos_r = jnp.repeat(jnp.arange(rows, dtype=F32), GRID_W)
    pos_c = jnp.tile(jnp.arange(GRID_W, dtype=F32), rows)
    inv = ROPE_BASE ** (-jnp.arange(ROPE_P, dtype=F32) / ROPE_P)
    ang_r = pos_r[:, None] * inv
    ang_c = pos_c[:, None] * inv
    zeros = jnp.zeros((seq, ROPE_P), F32)

    def axis_tabs(ang):
        cs, sn = jnp.cos(ang), jnp.sin(ang)
        return (jnp.concatenate([cs, cs], -1), jnp.concatenate([-sn, zeros], -1), jnp.concatenate([zeros, sn], -1))

    tr, tc = axis_tabs(ang_r), axis_tabs(ang_c)
    reps = LANE // (2 * ROPE_AXIS_DIM)
    return tuple(jnp.tile(jnp.concatenate([a, b], -1), (1, reps)) for a, b in zip(tr, tc))


MERGE_SUB = 128


def _merge_kernel(of_ref, ob_ref, ra_ref, od_ref, ga_ref, gb_ref, x_ref, mod_ref, gn_ref, wpa_ref, wpb_ref,
                  wout_ref, l1g_ref, l1b_ref, wr_ref, x1_ref, h2_ref, aff_ref, *, alpha):
    mod = mod_ref[0]
    tm = x_ref.shape[0]
    sub = min(tm, MERGE_SUB)
    blocks = [slice(i * sub, (i + 1) * sub) for i in range(tm // sub)]

    def gla_out(rows):
        o = of_ref[rows, :] + ob_ref[rows, :]
        parts = []
        for h in range(GLA_HEADS):
            oh = o[:, h * GLA_DV:(h + 1) * GLA_DV]
            parts.append(oh * lax.rsqrt(jnp.mean(oh * oh, axis=-1, keepdims=True) + EPS) * gn_ref[...])
        return (jnp.concatenate(parts, axis=-1) * _silu(ra_ref[rows, :])).astype(BF16)

    o_as = [gla_out(rows) for rows in blocks]
    pas = [jnp.dot(o_a, wpa_ref[...], preferred_element_type=F32) for o_a in o_as]
    pbs = [jnp.dot(od_ref[rows, :], wpb_ref[...], preferred_element_type=F32) for rows in blocks]
    mergeds = [(_sigmoid(ga_ref[rows, :]) * pa + _sigmoid(gb_ref[rows, :]) * pb).astype(BF16)
               for rows, pa, pb in zip(blocks, pas, pbs)]
    mixes = [jnp.dot(merged, wout_ref[...], preferred_element_type=F32) for merged in mergeds]
    for rows, mix in zip(blocks, mixes):
        x1 = _ln_plain(alpha * x_ref[rows, :] + mod[2:3] * mix) * l1g_ref[...] + l1b_ref[...]
        x1_ref[rows, :] = x1
        h2 = _ln_plain(x1) * (1.0 + mod[4:5]) + mod[3:4]
        h2_ref[rows, :] = h2.astype(h2_ref.dtype)
        logits = lax.dot_general(wr_ref[...], h2, (((1,), (1,)), ((), ())), precision=HIGHEST,
                                 preferred_element_type=F32)
        e = jnp.exp(logits - jnp.max(logits, axis=0, keepdims=True))
        aff_ref[:, rows] = e / jnp.sum(e, axis=0, keepdims=True)


def _merge(o_f, o_b, y, o_d, x, mod_all, mod_row, gn, wpa, wpb, wout, l1g, l1b, wr_t, tm, alpha):
    n, d = x.shape
    cb = lambda col: col // d
    rowspec = lambda col: pl.BlockSpec((tm, d), lambda i: (i, col))
    const = lambda shape: pl.BlockSpec(shape, lambda i: (0,) * len(shape))
    return pl.pallas_call(
        functools.partial(_merge_kernel, alpha=alpha),
        out_shape=(jax.ShapeDtypeStruct((n, d), F32), jax.ShapeDtypeStruct((n, d), F32),
                   jax.ShapeDtypeStruct((N_EXPERTS, n), F32)),
        grid=(n // tm,),
        in_specs=[rowspec(0), rowspec(0),
                  rowspec(cb(COL_RA)), rowspec(0), rowspec(cb(COL_GA)), rowspec(cb(COL_GB)), rowspec(0),
                  pl.BlockSpec((1, 6, d), lambda i: (mod_row(i, tm), 0, 0)),
                  const((1, GLA_DV)), const((d, d)), const((d, d)), const((d, d)),
                  const((1, d)), const((1, d)), const((N_EXPERTS, d))],
        out_specs=(pl.BlockSpec((tm, d), lambda i: (i, 0)), pl.BlockSpec((tm, d), lambda i: (i, 0)),
                   pl.BlockSpec((N_EXPERTS, tm), lambda i: (0, i))),
        compiler_params=_cparams(("parallel",)),
        name="merge",
    )(o_f, o_b, y, o_d, y, y, x, mod_all, gn, wpa, wpb, wout, l1g, l1b, wr_t)


def _moe_kernel(idx_ref, hp_hbm, hs_hbm, gate_ref, wg_ref, wu_ref, wd_ref, yp_ref, ys_ref, xf_scr, xb_scr, sem,
                *, rows, cap_p):
    e = pl.program_id(0)
    f = pl.program_id(1)
    nf = pl.num_programs(1)
    cap = xf_scr.shape[0]

    @pl.when(f == 0)
    def _():
        def issue_from(h_hbm):
            def issue(r, carry):
                tok = idx_ref[e * cap + r]
                pltpu.make_async_copy(h_hbm.at[pl.ds(tok, 1)], xf_scr.at[pl.ds(r, 1)], sem).start()
                return carry
            return issue

        lax.fori_loop(0, cap_p, issue_from(hp_hbm), 0, unroll=8)
        lax.fori_loop(cap_p, cap, issue_from(hs_hbm), 0, unroll=8)

        def drain(r, carry):
            pltpu.make_async_copy(hp_hbm.at[pl.ds(0, 1)], xf_scr.at[pl.ds(r, 1)], sem).wait()
            return carry

        lax.fori_loop(0, cap, drain, 0, unroll=8)
        xb_scr[...] = xf_scr[...].astype(BF16)

    wg = wg_ref[0].astype(BF16)
    wu = wu_ref[0].astype(BF16)
    wd = wd_ref[0].astype(BF16)
    for t in range(cap // rows):
        lo = t * rows
        out_ref, olo = (yp_ref, lo) if lo < cap_p else (ys_ref, lo - cap_p)
        xr = xb_scr[lo:lo + rows, :]
        hg = jnp.dot(xr, wg, preferred_element_type=F32)
        hu = jnp.dot(xr, wu, preferred_element_type=F32)
        hid = (_silu(hg) * hu).astype(BF16)
        prev = jnp.where(f > 0, out_ref[0, olo:olo + rows, :], 0.0)
        out_ref[0, olo:olo + rows, :] = prev + jnp.dot(hid, wd, preferred_element_type=F32)

    @pl.when(f == nf - 1)
    def _():
        yp_ref[0] = yp_ref[0] * gate_ref[0, 0:cap_p, :]
        ys_ref[0] = ys_ref[0] * gate_ref[0, cap_p:, :]


def _moe(idx, hp, hs, gate, wg, wu, wd, cap_p, ft, rows):
    e, cap = idx.shape
    cap_s = cap - cap_p
    d = hp.shape[1]
    ff = wg.shape[2]
    grid_spec = pltpu.PrefetchScalarGridSpec(
        num_scalar_prefetch=1,
        grid=(e, ff // ft),
        in_specs=[pl.BlockSpec(memory_space=pl.ANY),
                  pl.BlockSpec(memory_space=pl.ANY),
                  pl.BlockSpec((1, cap, 1), lambda i, f, idx_ref: (i, 0, 0)),
                  pl.BlockSpec((1, d, ft), lambda i, f, idx_ref: (i, 0, f)),
                  pl.BlockSpec((1, d, ft), lambda i, f, idx_ref: (i, 0, f)),
                  pl.BlockSpec((1, ft, d), lambda i, f, idx_ref: (i, f, 0))],
        out_specs=(pl.BlockSpec((1, cap_p, d), lambda i, f, idx_ref: (i, 0, 0)),
                   pl.BlockSpec((1, cap_s, d), lambda i, f, idx_ref: (i, 0, 0))),
        scratch_shapes=[pltpu.VMEM((cap, d), F32), pltpu.VMEM((cap, d), BF16), pltpu.SemaphoreType.DMA])
    return pl.pallas_call(
        functools.partial(_moe_kernel, rows=rows, cap_p=cap_p),
        out_shape=(jax.ShapeDtypeStruct((e, cap_p, d), F32), jax.ShapeDtypeStruct((e, cap_s, d), F32)),
        grid_spec=grid_spec,
        compiler_params=_cparams(("arbitrary", "arbitrary")),
        name="moe",
    )(idx.reshape(-1), hp, hs, gate, wg, wu, wd)


def _final_kernel(x1_ref, m_ref, mod_ref, g_ref, b_ref, o_ref, *, alpha):
    mod = mod_ref[0]
    o_ref[...] = _ln_plain(alpha * x1_ref[...] + mod[5:6] * m_ref[...]) * g_ref[...] + b_ref[...]


def _final(x1, moe_out, mod_all, mod_row, g, b, tm, alpha):
    n, d = x1.shape
    return pl.pallas_call(
        functools.partial(_final_kernel, alpha=alpha),
        out_shape=jax.ShapeDtypeStruct((n, d), F32),
        grid=(n // tm,),
        in_specs=[pl.BlockSpec((tm, d), lambda i: (i, 0)), pl.BlockSpec((tm, d), lambda i: (i, 0)),
                  pl.BlockSpec((1, 6, d), lambda i: (mod_row(i, tm), 0, 0)),
                  pl.BlockSpec((1, d), lambda i: (0, 0)), pl.BlockSpec((1, d), lambda i: (0, 0))],
        out_specs=pl.BlockSpec((tm, d), lambda i: (i, 0)),
        compiler_params=_cparams(("parallel",)),
        name="final",
    )(x1, moe_out, mod_all, g, b)


def _pick(n, pref):
    t = pref
    while n % t:
        t //= 2
    return t


def _mix_group(x3, mod_all, mod_row, lw, lam_init, alpha, s0, ctx):
    batch, seq, d = x3.shape
    n = batch * seq
    x = x3.reshape(n, d)
    tile_span = n if ctx is None else seq
    y, a = _in_proj(x, mod_all, mod_row, lw["w_main"], lw["w_a"], _pick(tile_span, 1024), 1024)
    o_f, s_f = _gla(y, a, lw["w_alpha_pad"][0], lw["b_alpha"][0], s0[0], batch, seq, False)
    o_b, s_b = _gla(y, a, lw["w_alpha_pad"][1], lw["b_alpha"][1], s0[1], batch, seq, True)
    if ctx is None:
        o_d = _diff_attn_ctx(y, lw["lam_vecs"], lw["subln_g"], batch, seq, lam_init)
    else:
        k_cache, v_cache = ctx
        past = k_cache.shape[0] // batch
        o_d = _diff_attn_lat(y, k_cache, v_cache, _rope_tables(seq), lw["lam_vecs"], lw["subln_g"],
                             batch, seq, past, _pick(seq, 1024), lam_init)
    tm = _pick(seq, 256)
    x1, h2, aff_t = _merge(o_f, o_b, y, o_d, x, mod_all, mod_row, lw["gla_norm_g"], lw["w_proj_a"], lw["w_proj_b"],
                           lw["w_out"], lw["ln1_g"], lw["ln1_b"], lw["w_router_t"], tm, alpha)
    gate, idx = lax.top_k(aff_t, CAPACITY_FACTOR * n // N_EXPERTS)
    return dict(x1=x1, h2=h2, gate=gate, idx=idx, y=y, states=(s_f, s_b), mod_row=mod_row,
                tm_final=_pick(tile_span, 512), shape=(batch, seq, d))


def _ffn_groups(gp, gs, mod_all, lw, alpha):
    cap_p, cap_s = gp["idx"].shape[1], gs["idx"].shape[1]
    idx = jnp.concatenate([gp["idx"], gs["idx"]], axis=1)
    gate = jnp.concatenate([gp["gate"], gs["gate"]], axis=1)[..., None]
    ye_p, ye_s = _moe(idx, gp["h2"], gs["h2"], gate, lw["w_exp_gate"], lw["w_exp_up"], lw["w_exp_down"],
                      cap_p, 256, _pick(math.gcd(cap_p, cap_s), 512))
    outs = []
    for g, ye in ((gp, ye_p), (gs, ye_s)):
        batch, seq, d = g["shape"]
        moe_out = jnp.zeros((batch * seq, d), F32).at[g["idx"].reshape(-1)].add(ye.reshape(-1, d))
        x2 = _final(g["x1"], moe_out, mod_all, g["mod_row"], lw["ln2_g"], lw["ln2_b"], g["tm_final"], alpha)
        outs.append(x2.reshape(batch, seq, d))
    return outs


def kernel(x_prompt, x_sample, cache_diff_k, cache_diff_v, state_gla_fwd, state_gla_bwd, c, c_ctx, w_mod, b_mod, w_in, w_alpha_f, b_alpha_f, w_alpha_b, b_alpha_b, gla_norm_g, w_proj_a, lambda_q1, lambda_k1, lambda_q2, lambda_k2, subln_g, w_proj_b, w_out, ln1_g, ln1_b, w_router, w_exp_gate, w_exp_up, w_exp_down, ln2_g, ln2_b):
    depth = w_in.shape[0]
    bp, sp, d = x_prompt.shape
    bs, ss, _ = x_sample.shape
    alpha = (2.0 * depth) ** 0.25
    xp, xs = x_prompt, x_sample

    n_mod = 1 + bs
    mod_rows = -(-n_mod // 8) * 8
    c_all = jnp.zeros((mod_rows, d), F32).at[0].set(c_ctx).at[1:n_mod].set(c)

    new_k, new_v, new_f, new_b = [], [], [], []
    for l in range(depth):
        lam_init = 0.8 - 0.6 * math.exp(-0.3 * l)
        mod_all = _modulation(c_all, w_mod[l], b_mod[l]).reshape(mod_rows, 6, d)
        wi = w_in[l]
        o = 0
        pieces = {}
        for name, width in zip(("qa", "ka", "va", "ra", "af", "ab", "qd", "kd", "vd", "ga", "gb"),
                               (GLA_QK, GLA_QK, GLA_VW, GLA_VW, GLA_RANK, GLA_RANK, d, d, d, d, d)):
            pieces[name] = wi[:, o:o + width]
            o += width
        w_main = jnp.concatenate([pieces[k] for k in ("qa", "ka", "va", "ra", "qd", "kd", "vd", "ga", "gb")],
                                 axis=1).astype(BF16)
        w_a = jnp.concatenate([pieces["af"], pieces["ab"], jnp.zeros((d, LANE - 2 * GLA_RANK), F32)],
                              axis=1).astype(BF16)
        w_alpha_pad = jnp.zeros((2, LANE, GLA_QK), F32)
        w_alpha_pad = w_alpha_pad.at[0, 0:GLA_RANK].set(w_alpha_f[l]).at[1, GLA_RANK:2 * GLA_RANK].set(w_alpha_b[l])
        lw = dict(
            w_main=w_main, w_a=w_a, w_alpha_pad=w_alpha_pad.astype(BF16),
            b_alpha=jnp.stack([b_alpha_f[l], b_alpha_b[l]])[:, None, :],
            lam_vecs=jnp.stack([lambda_q1[l], lambda_k1[l], lambda_q2[l], lambda_k2[l]]),
            subln_g=subln_g[l][None, :], gla_norm_g=gla_norm_g[l][None, :],
            w_proj_a=w_proj_a[l].astype(BF16), w_proj_b=w_proj_b[l].astype(BF16), w_out=w_out[l].astype(BF16),
            ln1_g=ln1_g[l][None, :], ln1_b=ln1_b[l][None, :], w_router_t=w_router[l].T,
            w_exp_gate=w_exp_gate[l], w_exp_up=w_exp_up[l], w_exp_down=w_exp_down[l],
            ln2_g=ln2_g[l][None, :], ln2_b=ln2_b[l][None, :])

        zeros = jnp.zeros((2, bp, GLA_HEADS, GLA_DK, GLA_DV), F32)
        gp = _mix_group(xp, mod_all, lambda i, tm: 0, lw, lam_init, alpha, zeros, None)
        y_p = gp["y"]
        new_k.append(y_p[:, COL_KD:COL_KD + d].reshape(bp, sp, DIFF_HEADS, 2, DIFF_DH))
        new_v.append(y_p[:, COL_VD:COL_VD + d].reshape(bp, sp, DIFF_HEADS, 2 * DIFF_DH))
        new_f.append(gp["states"][0])
        new_b.append(gp["states"][1])

        s0 = (state_gla_fwd[:, l], state_gla_bwd[:, l])
        past = cache_diff_k.shape[2]
        ctx = (cache_diff_k[:, l].reshape(bs * past, d), cache_diff_v[:, l].reshape(bs * past, d))
        gs = _mix_group(xs, mod_all, lambda i, tm: 1 + (i * tm) // ss, lw, lam_init, alpha, s0, ctx)
        xp, xs = _ffn_groups(gp, gs, mod_all, lw, alpha)

    return (xp, xs, jnp.stack(new_k, axis=1), jnp.stack(new_v, axis=1),
            jnp.stack(new_f, axis=1), jnp.stack(new_b, axis=1))
```

```python
import functools
import math

import jax
import jax.numpy as jnp
from jax import lax
from jax.experimental import pallas as pl
from jax.experimental.pallas import tpu as pltpu

F32 = jnp.float32
BF16 = jnp.bfloat16
HIGHEST = lax.Precision.HIGHEST

LANE = 128
EPS = 1e-6
GRID_W = 64
GLA_HEADS = 4
GLA_DK = 128
GLA_DV = 256
GLA_QK = GLA_HEADS * GLA_DK
GLA_VW = GLA_HEADS * GLA_DV
GLA_RANK = 16
GLA_TAU = 16.0
CHUNK = 64
DIFF_HEADS = 8
DIFF_DH = 64
ROPE_BASE = 10000.0
ROPE_AXIS_DIM = DIFF_DH // 2
ROPE_P = ROPE_AXIS_DIM // 2
N_EXPERTS = 16
CAPACITY_FACTOR = 2
VMEM_LIMIT = 56 * 1024 * 1024
COMBINE_TM = 256
WIN = 64

COL_QA, COL_KA, COL_VA, COL_RA, COL_QD, COL_KD, COL_VD, COL_GA, COL_GB = (
    0, 512, 1024, 2048, 3072, 4096, 5120, 6144, 7168)
MAIN_W = 8192


def _cparams(sem):
    return pltpu.CompilerParams(dimension_semantics=sem, vmem_limit_bytes=VMEM_LIMIT)


def _ln_plain(x):
    mu = jnp.mean(x, axis=-1, keepdims=True)
    xc = x - mu
    var = jnp.mean(xc * xc, axis=-1, keepdims=True)
    return xc * lax.rsqrt(var + EPS)


def _sigmoid(x):
    return 1.0 / (1.0 + jnp.exp(-x))


def _silu(x):
    return x * _sigmoid(x)


def _mod_kernel(c_ref, w_ref, b_ref, o_ref):
    c = c_ref[...]
    o_ref[...] = jnp.dot(_silu(c), w_ref[...], precision=HIGHEST, preferred_element_type=F32) + b_ref[...]


def _modulation(c_all, w_mod, b_mod):
    rows, d = c_all.shape
    n = w_mod.shape[1]
    tn = 1024
    return pl.pallas_call(
        _mod_kernel,
        out_shape=jax.ShapeDtypeStruct((rows, n), F32),
        grid=(n // tn,),
        in_specs=[pl.BlockSpec((rows, d), lambda j: (0, 0)),
                  pl.BlockSpec((d, tn), lambda j: (0, j)),
                  pl.BlockSpec((1, tn), lambda j: (0, j))],
        out_specs=pl.BlockSpec((rows, tn), lambda j: (0, j)),
        compiler_params=_cparams(("parallel",)),
        name="modulation",
    )(c_all, w_mod, b_mod.reshape(1, n))


def _in_proj_kernel(x_ref, mod_ref, w_ref, wa_ref, y_ref, a_ref, h_scr):
    @pl.when(pl.program_id(1) == 0)
    def _():
        mod = mod_ref[0]
        h = _ln_plain(x_ref[...]) * (1.0 + mod[1:2]) + mod[0:1]
        hb = h.astype(BF16)
        h_scr[...] = hb
        a_ref[...] = jnp.dot(hb, wa_ref[...], preferred_element_type=F32)

    y_ref[...] = jnp.dot(h_scr[...], w_ref[...], preferred_element_type=F32)


def _in_proj(x, mod_all, mod_row, w_main, w_a, tm, tn):
    n, d = x.shape
    nw = w_main.shape[1]
    return pl.pallas_call(
        _in_proj_kernel,
        out_shape=(jax.ShapeDtypeStruct((n, nw), F32), jax.ShapeDtypeStruct((n, LANE), F32)),
        grid=(n // tm, nw // tn),
        in_specs=[pl.BlockSpec((tm, d), lambda i, j: (i, 0)),
                  pl.BlockSpec((1, 6, d), lambda i, j: (mod_row(i, tm), 0, 0)),
                  pl.BlockSpec((d, tn), lambda i, j: (0, j)),
                  pl.BlockSpec((d, LANE), lambda i, j: (0, 0))],
        out_specs=(pl.BlockSpec((tm, tn), lambda i, j: (i, j)),
                   pl.BlockSpec((tm, LANE), lambda i, j: (i, 0))),
        scratch_shapes=[pltpu.VMEM((tm, d), BF16)],
        compiler_params=_cparams(("parallel", "arbitrary")),
        name="in_proj",
    )(x, mod_all, w_main, w_a)


def _gla_kernel(a_ref, q_ref, k_ref, v_ref, wal_ref, bal_ref, s0_ref, o_ref, sout_ref, s_scr, *, rev, nch):
    step = pl.program_id(1)

    @pl.when(step == 0)
    def _():
        s_scr[...] = s0_ref[0]

    z = jnp.dot(a_ref[...].astype(BF16), wal_ref[...], preferred_element_type=F32) + bal_ref[...]
    g = (jnp.minimum(z, 0.0) - jnp.log(1.0 + jnp.exp(-jnp.abs(z)))) * (1.0 / GLA_TAU)
    row = lax.broadcasted_iota(jnp.int32, (CHUNK, CHUNK), 0)
    col = lax.broadcasted_iota(jnp.int32, (CHUNK, CHUNK), 1)
    keep = (col >= row) if rev else (col <= row)
    tri = keep.astype(F32)
    chunks = list(reversed(range(nch)) if rev else range(nch))
    heads = [(slice(h * GLA_DK, (h + 1) * GLA_DK), slice(h * GLA_DV, (h + 1) * GLA_DV)) for h in range(GLA_HEADS)]
    pre = {}
    for ci in chunks:
        rs = slice(ci * CHUNK, (ci + 1) * CHUNK)
        gc = g[rs]
        b = jnp.dot(tri, gc, precision=HIGHEST, preferred_element_type=F32)
        b_last = jnp.sum(gc, axis=0, keepdims=True)
        eb_last_t = jnp.transpose(jnp.broadcast_to(jnp.exp(b_last), (8, GLA_QK)))
        k = k_ref[rs, :]
        pre[ci] = dict(rs=rs, eb_last_t=eb_last_t,
                       qe=(q_ref[rs, :] * (GLA_DK ** -0.5) * jnp.exp(b)).astype(BF16),
                       ke=(k * jnp.exp(-b)).astype(BF16),
                       kd=(k * jnp.exp(b_last - b)).astype(BF16),
                       v=v_ref[rs, :].astype(BF16))
    for ci in chunks:
        c = pre[ci]
        c["a"] = [jnp.where(keep, lax.dot_general(c["qe"][:, ks], c["ke"][:, ks], (((1,), (1,)), ((), ())),
                                                  preferred_element_type=F32), 0.0).astype(BF16)
                  for ks, _ in heads]
    for ci in chunks:
        c = pre[ci]
        c["o"] = [jnp.dot(c["a"][h], c["v"][:, vs], preferred_element_type=F32) for h, (_, vs) in enumerate(heads)]
        c["upd"] = [lax.dot_general(c["kd"][:, ks], c["v"][:, vs], (((0,), (0,)), ((), ())),
                                    preferred_element_type=F32) for ks, vs in heads]
    state = [s_scr[h] for h in range(GLA_HEADS)]
    for ci in chunks:
        c = pre[ci]
        for h, (ks, vs) in enumerate(heads):
            o_ref[c["rs"], vs] = c["o"][h] + jnp.dot(c["qe"][:, ks], state[h].astype(BF16),
                                                     preferred_element_type=F32)
            state[h] = c["eb_last_t"][ks, 0:1] * state[h] + c["upd"][h]
    for h in range(GLA_HEADS):
        s_scr[h] = state[h]

    @pl.when(step == pl.num_programs(1) - 1)
    def _():
        sout_ref[0] = s_scr[...]


def _gla(y, a, w_alpha_pad, b_alpha, s0, batch, seq, rev):
    n = y.shape[0]
    nch = min(seq // CHUNK, 8)
    rows = nch * CHUNK
    ns = seq // rows

    def rowblk(b, s):
        return b * ns + (ns - 1 - s if rev else s)

    state_spec = pl.BlockSpec((1, GLA_HEADS, GLA_DK, GLA_DV), lambda b, s: (b, 0, 0, 0))
    return pl.pallas_call(
        functools.partial(_gla_kernel, rev=rev, nch=nch),
        out_shape=(jax.ShapeDtypeStruct((n, GLA_VW), F32),
                   jax.ShapeDtypeStruct((batch, GLA_HEADS, GLA_DK, GLA_DV), F32)),
        grid=(batch, ns),
        in_specs=[pl.BlockSpec((rows, LANE), lambda b, s: (rowblk(b, s), 0)),
                  pl.BlockSpec((rows, GLA_QK), lambda b, s: (rowblk(b, s), COL_QA // GLA_QK)),
                  pl.BlockSpec((rows, GLA_QK), lambda b, s: (rowblk(b, s), COL_KA // GLA_QK)),
                  pl.BlockSpec((rows, GLA_VW), lambda b, s: (rowblk(b, s), COL_VA // GLA_VW)),
                  pl.BlockSpec((LANE, GLA_QK), lambda b, s: (0, 0)),
                  pl.BlockSpec((1, GLA_QK), lambda b, s: (0, 0)),
                  state_spec],
        out_specs=(pl.BlockSpec((rows, GLA_VW), lambda b, s: (rowblk(b, s), 0)), state_spec),
        scratch_shapes=[pltpu.VMEM((GLA_HEADS, GLA_DK, GLA_DV), F32)],
        compiler_params=_cparams(("parallel", "arbitrary")),
        name="gla_bwd" if rev else "gla_fwd",
    )(a, y, y, y, w_alpha_pad, b_alpha, s0)


def _rope(x, cos_t, sin_a, sin_b):
    return x * cos_t + pltpu.roll(x, LANE - ROPE_P, 1) * sin_a + pltpu.roll(x, ROPE_P, 1) * sin_b


def _lambda(lam_ref, lam_init):
    lq1, lk1, lq2, lk2 = lam_ref[0:1], lam_ref[1:2], lam_ref[2:3], lam_ref[3:4]
    return (jnp.exp(jnp.sum(lq1 * lk1, axis=-1, keepdims=True))
            - jnp.exp(jnp.sum(lq2 * lk2, axis=-1, keepdims=True)) + lam_init)


ATTN_SUB = 128


def _attn_core(q, k, v, lam, g_ref, o_ref, lam_init):
    tq = q.shape[0]
    sub = min(tq, ATTN_SUB)
    lane = lax.broadcasted_iota(jnp.int32, (sub, LANE), 1)
    blocks = [slice(i * sub, (i + 1) * sub) for i in range(tq // sub)]
    s2s = []
    for rows in blocks:
        qs = q[rows] * (DIFF_DH ** -0.5 * math.log2(math.e))
        q2 = jnp.concatenate([jnp.where(lane < DIFF_DH, qs, 0.0), jnp.where(lane >= DIFF_DH, qs, 0.0)],
                             axis=0).astype(BF16)
        s2s.append(lax.dot_general(q2, k, (((1,), (1,)), ((), ())), preferred_element_type=F32))
    ps = [jnp.exp2(s2 - jnp.max(s2, axis=-1, keepdims=True)) for s2 in s2s]
    rs = [1.0 / jnp.sum(p, axis=-1, keepdims=True) for p in ps]
    pvs = [jnp.dot(p.astype(BF16), v, preferred_element_type=F32) for p in ps]
    for rows, pv, r in zip(blocks, pvs, rs):
        pv = pv * r
        o = pv[:sub] - lam * pv[sub:]
        ms = jnp.mean(o * o, axis=-1, keepdims=True)
        o_ref[rows, :] = (o * lax.rsqrt(ms + EPS) * g_ref[...] * (1.0 - lam_init)).astype(o_ref.dtype)


def _attn_ctx_kernel(q_ref, k_ref, v_ref, lam_ref, g_ref, o_ref, *, lam_init):
    lam = _lambda(lam_ref, lam_init)
    _attn_core(q_ref[...], k_ref[...].astype(BF16), v_ref[...].astype(BF16), lam, g_ref, o_ref, lam_init)


def _attn_lat_kernel(q_ref, k_ref, v_ref, kc_ref, vc_ref, cq_ref, saq_ref, sbq_ref, ck_ref, sak_ref, sbk_ref,
                     lam_ref, g_ref, o_ref, k_scr, v_scr, *, seq, lam_init):
    @pl.when(pl.program_id(2) == 0)
    def _():
        k_scr[0:seq] = _rope(k_ref[...], ck_ref[...], sak_ref[...], sbk_ref[...]).astype(BF16)
        k_scr[seq:] = kc_ref[...].astype(BF16)
        v_scr[0:seq] = v_ref[...].astype(BF16)
        v_scr[seq:] = vc_ref[...].astype(BF16)

    lam = _lambda(lam_ref, lam_init)
    q = _rope(q_ref[...], cq_ref[...], saq_ref[...], sbq_ref[...])
    _attn_core(q, k_scr[...], v_scr[...], lam, g_ref, o_ref, lam_init)


def _diff_attn_ctx(y, lam_vecs, subln_g, batch, seq, lam_init):
    n = y.shape[0]
    tq = seq
    hb = lambda col: col // LANE
    return pl.pallas_call(
        functools.partial(_attn_ctx_kernel, lam_init=lam_init),
        out_shape=jax.ShapeDtypeStruct((n, DIFF_HEADS * LANE), BF16),
        grid=(batch, DIFF_HEADS),
        in_specs=[pl.BlockSpec((tq, LANE), lambda b, h: (b, hb(COL_QD) + h)),
                  pl.BlockSpec((seq, LANE), lambda b, h: (b, hb(COL_KD) + h)),
                  pl.BlockSpec((seq, LANE), lambda b, h: (b, hb(COL_VD) + h)),
                  pl.BlockSpec((4, DIFF_DH), lambda b, h: (0, 0)),
                  pl.BlockSpec((1, LANE), lambda b, h: (0, 0))],
        out_specs=pl.BlockSpec((tq, LANE), lambda b, h: (b, h)),
        compiler_params=_cparams(("parallel", "parallel")),
        name="diff_attn_ctx",
    )(y, y, y, lam_vecs, subln_g)


def _diff_attn_lat(y, k_cache, v_cache, tabs, lam_vecs, subln_g, batch, seq, past, tq, lam_init):
    n = y.shape[0]
    nq = seq // tq
    hb = lambda col: col // LANE
    cos_t, sin_a, sin_b = tabs
    qtab = pl.BlockSpec((tq, LANE), lambda b, h, i: (i, 0))
    ktab = pl.BlockSpec((seq, LANE), lambda b, h, i: (0, 0))
    return pl.pallas_call(
        functools.partial(_attn_lat_kernel, seq=seq, lam_init=lam_init),
        out_shape=jax.ShapeDtypeStruct((n, DIFF_HEADS * LANE), BF16),
        grid=(batch, DIFF_HEADS, nq),
        in_specs=[pl.BlockSpec((tq, LANE), lambda b, h, i: (b * nq + i, hb(COL_QD) + h)),
                  pl.BlockSpec((seq, LANE), lambda b, h, i: (b, hb(COL_KD) + h)),
                  pl.BlockSpec((seq, LANE), lambda b, h, i: (b, hb(COL_VD) + h)),
                  pl.BlockSpec((past, LANE), lambda b, h, i: (b, h)),
                  pl.BlockSpec((past, LANE), lambda b, h, i: (b, h)),
                  qtab, qtab, qtab, ktab, ktab, ktab,
                  pl.BlockSpec((4, DIFF_DH), lambda b, h, i: (0, 0)),
                  pl.BlockSpec((1, LANE), lambda b, h, i: (0, 0))],
        out_specs=pl.BlockSpec((tq, LANE), lambda b, h, i: (b * nq + i, h)),
        scratch_shapes=[pltpu.VMEM((seq + past, LANE), BF16), pltpu.VMEM((seq + past, LANE), BF16)],
        compiler_params=_cparams(("parallel", "parallel", "arbitrary")),
        name="diff_attn_lat",
    )(y, y, y, k_cache, v_cache, cos_t, sin_a, sin_b, cos_t, sin_a, sin_b, lam_vecs, subln_g)


def _rope_tables(seq):
    rows = seq // GRID_W
    pos_r = jnp.repeat(jnp.arange(rows, dtype=F32), GRID_W)
    pos_c = jnp.tile(jnp.arange(GRID_W, dtype=F32), rows)
    inv = ROPE_BASE ** (-jnp.arange(ROPE_P, dtype=F32) / ROPE_P)
    ang_r = pos_r[:, None] * inv
    ang_c = pos_c[:, None] * inv
    zeros = jnp.zeros((seq, ROPE_P), F32)

    def axis_tabs(ang):
        cs, sn = jnp.cos(ang), jnp.sin(ang)
        return (jnp.concatenate([cs, cs], -1), jnp.concatenate([-sn, zeros], -1), jnp.concatenate([zeros, sn], -1))

    tr, tc = axis_tabs(ang_r), axis_tabs(ang_c)
    reps = LANE // (2 * ROPE_AXIS_DIM)
    return tuple(jnp.tile(jnp.concatenate([a, b], -1), (1, reps)) for a, b in zip(tr, tc))


MERGE_SUB = 128


def _merge_kernel(of_ref, ob_ref, ra_ref, od_ref, ga_ref, gb_ref, x_ref, mod_ref, gn_ref, wpa_ref, wpb_ref,
                  wout_ref, l1g_ref, l1b_ref, wr_ref, x1_ref, h2_ref, aff_ref, *, alpha):
    mod = mod_ref[0]
    tm = x_ref.shape[0]
    sub = min(tm, MERGE_SUB)
    blocks = [slice(i * sub, (i + 1) * sub) for i in range(tm // sub)]

    def gla_out(rows):
        o = of_ref[rows, :] + ob_ref[rows, :]
        parts = []
        for h in range(GLA_HEADS):
            oh = o[:, h * GLA_DV:(h + 1) * GLA_DV]
            parts.append(oh * lax.rsqrt(jnp.mean(oh * oh, axis=-1, keepdims=True) + EPS) * gn_ref[...])
        return (jnp.concatenate(parts, axis=-1) * _silu(ra_ref[rows, :])).astype(BF16)

    o_as = [gla_out(rows) for rows in blocks]
    pas = [jnp.dot(o_a, wpa_ref[...], preferred_element_type=F32) for o_a in o_as]
    pbs = [jnp.dot(od_ref[rows, :], wpb_ref[...], preferred_element_type=F32) for rows in blocks]
    mergeds = [(_sigmoid(ga_ref[rows, :]) * pa + _sigmoid(gb_ref[rows, :]) * pb).astype(BF16)
               for rows, pa, pb in zip(blocks, pas, pbs)]
    mixes = [jnp.dot(merged, wout_ref[...], preferred_element_type=F32) for merged in mergeds]
    for rows, mix in zip(blocks, mixes):
        x1 = _ln_plain(alpha * x_ref[rows, :] + mod[2:3] * mix) * l1g_ref[...] + l1b_ref[...]
        x1_ref[rows, :] = x1
        h2 = _ln_plain(x1) * (1.0 + mod[4:5]) + mod[3:4]
        h2_ref[rows, :] = h2.astype(h2_ref.dtype)
        logits = lax.dot_general(wr_ref[...], h2, (((1,), (1,)), ((), ())), precision=HIGHEST,
                                 preferred_element_type=F32)
        e = jnp.exp(logits - jnp.max(logits, axis=0, keepdims=True))
        aff_ref[:, rows] = e / jnp.sum(e, axis=0, keepdims=True)


def _merge(o_f, o_b, y, o_d, x, mod_all, mod_row, gn, wpa, wpb, wout, l1g, l1b, wr_t, tm, alpha):
    n, d = x.shape
    cb = lambda col: col // d
    rowspec = lambda col: pl.BlockSpec((tm, d), lambda i: (i, col))
    const = lambda shape: pl.BlockSpec(shape, lambda i: (0,) * len(shape))
    return pl.pallas_call(
        functools.partial(_merge_kernel, alpha=alpha),
        out_shape=(jax.ShapeDtypeStruct((n, d), F32), jax.ShapeDtypeStruct((n, d), F32),
                   jax.ShapeDtypeStruct((N_EXPERTS, n), F32)),
        grid=(n // tm,),
        in_specs=[rowspec(0), rowspec(0),
                  rowspec(cb(COL_RA)), rowspec(0), rowspec(cb(COL_GA)), rowspec(cb(COL_GB)), rowspec(0),
                  pl.BlockSpec((1, 6, d), lambda i: (mod_row(i, tm), 0, 0)),
                  const((1, GLA_DV)), const((d, d)), const((d, d)), const((d, d)),
                  const((1, d)), const((1, d)), const((N_EXPERTS, d))],
        out_specs=(pl.BlockSpec((tm, d), lambda i: (i, 0)), pl.BlockSpec((tm, d), lambda i: (i, 0)),
                   pl.BlockSpec((N_EXPERTS, tm), lambda i: (0, i))),
        compiler_params=_cparams(("parallel",)),
        name="merge",
    )(o_f, o_b, y, o_d, y, y, x, mod_all, gn, wpa, wpb, wout, l1g, l1b, wr_t)


def _moe_kernel(idx_ref, hp_hbm, hs_hbm, gate_ref, wg_ref, wu_ref, wd_ref, yp_ref, ys_ref, xf_scr, xb_scr, sem,
                *, rows, cap_p):
    e = pl.program_id(0)
    f = pl.program_id(1)
    nf = pl.num_programs(1)
    cap = xf_scr.shape[0]

    @pl.when(f == 0)
    def _():
        def issue_from(h_hbm):
            def issue(r, carry):
                tok = idx_ref[e * cap + r]
                pltpu.make_async_copy(h_hbm.at[pl.ds(tok, 1)], xf_scr.at[pl.ds(r, 1)], sem).start()
                return carry
            return issue

        lax.fori_loop(0, cap_p, issue_from(hp_hbm), 0, unroll=8)
        lax.fori_loop(cap_p, cap, issue_from(hs_hbm), 0, unroll=8)

        def drain(r, carry):
            pltpu.make_async_copy(hp_hbm.at[pl.ds(0, 1)], xf_scr.at[pl.ds(r, 1)], sem).wait()
            return carry

        lax.fori_loop(0, cap, drain, 0, unroll=8)
        xb_scr[...] = xf_scr[...].astype(BF16)

    wg = wg_ref[0].astype(BF16)
    wu = wu_ref[0].astype(BF16)
    wd = wd_ref[0].astype(BF16)
    for t in range(cap // rows):
        lo = t * rows
        out_ref, olo = (yp_ref, lo) if lo < cap_p else (ys_ref, lo - cap_p)
        xr = xb_scr[lo:lo + rows, :]
        hg = jnp.dot(xr, wg, preferred_element_type=F32)
        hu = jnp.dot(xr, wu, preferred_element_type=F32)
        hid = (_silu(hg) * hu).astype(BF16)
        prev = jnp.where(f > 0, out_ref[0, olo:olo + rows, :], 0.0)
        out_ref[0, olo:olo + rows, :] = prev + jnp.dot(hid, wd, preferred_element_type=F32)

    @pl.when(f == nf - 1)
    def _():
        cap_s = cap - cap_p
        yp_ref[0, 0:cap_p, :] = yp_ref[0, 0:cap_p, :] * gate_ref[0, 0:cap_p, :]
        ys_ref[0, 0:cap_s, :] = ys_ref[0, 0:cap_s, :] * gate_ref[0, cap_p:, :]
        yp_ref[0, cap_p:, :] = jnp.zeros((WIN, yp_ref.shape[2]), F32)
        ys_ref[0, cap_s:, :] = jnp.zeros((WIN, ys_ref.shape[2]), F32)


def _moe(idx, hp, hs, gate, wg, wu, wd, cap_p, ft, rows):
    e, cap = idx.shape
    cap_s = cap - cap_p
    d = hp.shape[1]
    ff = wg.shape[2]
    grid_spec = pltpu.PrefetchScalarGridSpec(
        num_scalar_prefetch=1,
        grid=(e, ff // ft),
        in_specs=[pl.BlockSpec(memory_space=pl.ANY),
                  pl.BlockSpec(memory_space=pl.ANY),
                  pl.BlockSpec((1, cap, 1), lambda i, f, idx_ref: (i, 0, 0)),
                  pl.BlockSpec((1, d, ft), lambda i, f, idx_ref: (i, 0, f)),
                  pl.BlockSpec((1, d, ft), lambda i, f, idx_ref: (i, 0, f)),
                  pl.BlockSpec((1, ft, d), lambda i, f, idx_ref: (i, f, 0))],
        out_specs=(pl.BlockSpec((1, cap_p + WIN, d), lambda i, f, idx_ref: (i, 0, 0)),
                   pl.BlockSpec((1, cap_s + WIN, d), lambda i, f, idx_ref: (i, 0, 0))),
        scratch_shapes=[pltpu.VMEM((cap, d), F32), pltpu.VMEM((cap, d), BF16), pltpu.SemaphoreType.DMA])
    return pl.pallas_call(
        functools.partial(_moe_kernel, rows=rows, cap_p=cap_p),
        out_shape=(jax.ShapeDtypeStruct((e, cap_p + WIN, d), F32), jax.ShapeDtypeStruct((e, cap_s + WIN, d), F32)),
        grid_spec=grid_spec,
        compiler_params=_cparams(("arbitrary", "arbitrary")),
        name="moe",
    )(idx.reshape(-1), hp, hs, gate, wg, wu, wd)


def _combine_kernel(base_ref, npass_ref, x1_ref, pos_ref, mod_ref, g_ref, b_ref, ye_hbm, o_ref,
                    ybuf, xbuf, sem, xsem, *, alpha):
    t = pl.program_id(0)
    nt = pl.num_programs(0)
    n_exp = pos_ref.shape[1]
    tm = x1_ref.shape[0]
    cap = ye_hbm.shape[1] - WIN

    def window(tile, e, p, dst, s):
        start = pl.multiple_of(jnp.minimum(base_ref[e * nt + tile] + p * WIN, cap), 8)
        return pltpu.make_async_copy(ye_hbm.at[e, pl.ds(start, WIN), :], dst.at[pl.ds(e * WIN, WIN), :], s)

    def start_first_pass(tile, slot):
        for e in range(n_exp):
            window(tile, e, 0, ybuf.at[slot], sem.at[slot]).start()

    def expand(yw, p):
        lane = lax.broadcasted_iota(jnp.int32, (tm, LANE), 1)
        groups = []
        for gi in range(n_exp * WIN // LANE):
            ea, eb = 2 * gi, 2 * gi + 1
            ra = pos_ref[:, ea:ea + 1] - (base_ref[ea * nt + t] + p * WIN)
            rb = pos_ref[:, eb:eb + 1] - (base_ref[eb * nt + t] + p * WIN) + WIN
            groups.append(jnp.where(lane == jnp.where(lane < WIN, ra, rb), 1.0, 0.0).astype(BF16))
        onehot = jnp.concatenate(groups, axis=1)
        hi = yw.astype(BF16)
        lo = (yw - hi.astype(F32)).astype(BF16)
        return (jnp.dot(onehot, hi, preferred_element_type=F32) + jnp.dot(onehot, lo, preferred_element_type=F32))

    @pl.when(t == 0)
    def _():
        start_first_pass(0, 0)

    slot = t % 2
    for e in range(n_exp):
        window(t, e, 0, ybuf.at[slot], sem.at[slot]).wait()

    @pl.when(t + 1 < nt)
    def _():
        start_first_pass(t + 1, 1 - slot)

    acc = expand(ybuf[slot], 0)

    def extra_pass(p, acc):
        for e in range(n_exp):
            window(t, e, p, xbuf, xsem).start()
        for e in range(n_exp):
            window(t, e, p, xbuf, xsem).wait()
        return acc + expand(xbuf[...], p)

    acc = lax.fori_loop(1, npass_ref[t], extra_pass, acc)
    mod = mod_ref[0]
    o_ref[...] = _ln_plain(alpha * x1_ref[...] + mod[5:6] * acc) * g_ref[...] + b_ref[...]


def _combine(x1, pos_t, base, npass, ye, mod_all, mod_row, g, b, alpha):
    n, d = x1.shape
    n_exp = pos_t.shape[1]
    tm = COMBINE_TM
    grid_spec = pltpu.PrefetchScalarGridSpec(
        num_scalar_prefetch=2,
        grid=(n // tm,),
        in_specs=[pl.BlockSpec((tm, d), lambda i, *_: (i, 0)),
                  pl.BlockSpec((tm, n_exp), lambda i, *_: (i, 0)),
                  pl.BlockSpec((1, 6, d), lambda i, *_: (mod_row(i, tm), 0, 0)),
                  pl.BlockSpec((1, d), lambda i, *_: (0, 0)), pl.BlockSpec((1, d), lambda i, *_: (0, 0)),
                  pl.BlockSpec(memory_space=pl.ANY)],
        out_specs=pl.BlockSpec((tm, d), lambda i, *_: (i, 0)),
        scratch_shapes=[pltpu.VMEM((2, n_exp * WIN, d), F32), pltpu.VMEM((n_exp * WIN, d), F32),
                        pltpu.SemaphoreType.DMA((2,)), pltpu.SemaphoreType.DMA])
    return pl.pallas_call(
        functools.partial(_combine_kernel, alpha=alpha),
        out_shape=jax.ShapeDtypeStruct((n, d), F32),
        grid_spec=grid_spec,
        compiler_params=_cparams(("arbitrary",)),
        name="combine",
    )(base.reshape(-1), npass, x1, pos_t, mod_all, g, b, ye)


def _slot_tables(idx_sorted, n):
    n_exp, cap = idx_sorted.shape
    nt = n // COMBINE_TM
    bounds = jnp.arange(nt + 1, dtype=jnp.int32) * COMBINE_TM
    off = jnp.sum(idx_sorted[:, None, :] < bounds[None, :, None], axis=-1, dtype=jnp.int32)
    base = (off[:, :-1] // 8) * 8
    span = off[:, 1:] - base
    npass = jnp.maximum(jnp.max(-(-span // WIN), axis=0), 1).astype(jnp.int32)
    e_ids = jnp.broadcast_to(jnp.arange(n_exp, dtype=jnp.int32)[:, None], (n_exp, cap))
    slots = jnp.broadcast_to(jnp.arange(cap, dtype=jnp.int32)[None, :], (n_exp, cap))
    pos_t = jnp.full((n, n_exp), -1, jnp.int32).at[idx_sorted, e_ids].set(
        slots, unique_indices=True, indices_are_sorted=False, mode="promise_in_bounds")
    return pos_t, base, npass


def _pick(n, pref):
    t = pref
    while n % t:
        t //= 2
    return t


def _mix_group(x3, mod_all, mod_row, lw, lam_init, alpha, s0, ctx):
    batch, seq, d = x3.shape
    n = batch * seq
    x = x3.reshape(n, d)
    tile_span = n if ctx is None else seq
    y, a = _in_proj(x, mod_all, mod_row, lw["w_main"], lw["w_a"], _pick(tile_span, 1024), 1024)
    o_f, s_f = _gla(y, a, lw["w_alpha_pad"][0], lw["b_alpha"][0], s0[0], batch, seq, False)
    o_b, s_b = _gla(y, a, lw["w_alpha_pad"][1], lw["b_alpha"][1], s0[1], batch, seq, True)
    if ctx is None:
        o_d = _diff_attn_ctx(y, lw["lam_vecs"], lw["subln_g"], batch, seq, lam_init)
    else:
        k_cache, v_cache = ctx
        past = k_cache.shape[0] // batch
        o_d = _diff_attn_lat(y, k_cache, v_cache, _rope_tables(seq), lw["lam_vecs"], lw["subln_g"],
                             batch, seq, past, _pick(seq, 1024), lam_init)
    tm = _pick(seq, 256)
    x1, h2, aff_t = _merge(o_f, o_b, y, o_d, x, mod_all, mod_row, lw["gla_norm_g"], lw["w_proj_a"], lw["w_proj_b"],
                           lw["w_out"], lw["ln1_g"], lw["ln1_b"], lw["w_router_t"], tm, alpha)
    _, idx = lax.top_k(aff_t, CAPACITY_FACTOR * n // N_EXPERTS)
    idx = jnp.sort(idx, axis=1)
    gate = jnp.take_along_axis(aff_t, idx, axis=1)
    return dict(x1=x1, h2=h2, gate=gate, idx=idx, y=y, states=(s_f, s_b), mod_row=mod_row, shape=(batch, seq, d))


def _ffn_groups(gp, gs, mod_all, lw, alpha):
    cap_p, cap_s = gp["idx"].shape[1], gs["idx"].shape[1]
    idx = jnp.concatenate([gp["idx"], gs["idx"]], axis=1)
    gate = jnp.concatenate([gp["gate"], gs["gate"]], axis=1)[..., None]
    ye_p, ye_s = _moe(idx, gp["h2"], gs["h2"], gate, lw["w_exp_gate"], lw["w_exp_up"], lw["w_exp_down"],
                      cap_p, 256, _pick(math.gcd(cap_p, cap_s), 512))
    outs = []
    for g, ye in ((gp, ye_p), (gs, ye_s)):
        batch, seq, d = g["shape"]
        pos_t, base, npass = _slot_tables(g["idx"], batch * seq)
        x2 = _combine(g["x1"], pos_t, base, npass, ye, mod_all, g["mod_row"], lw["ln2_g"], lw["ln2_b"], alpha)
        outs.append(x2.reshape(batch, seq, d))
    return outs


def kernel(x_prompt, x_sample, cache_diff_k, cache_diff_v, state_gla_fwd, state_gla_bwd, c, c_ctx, w_mod, b_mod, w_in, w_alpha_f, b_alpha_f, w_alpha_b, b_alpha_b, gla_norm_g, w_proj_a, lambda_q1, lambda_k1, lambda_q2, lambda_k2, subln_g, w_proj_b, w_out, ln1_g, ln1_b, w_router, w_exp_gate, w_exp_up, w_exp_down, ln2_g, ln2_b):
    depth = w_in.shape[0]
    bp, sp, d = x_prompt.shape
    bs, ss, _ = x_sample.shape
    alpha = (2.0 * depth) ** 0.25
    xp, xs = x_prompt, x_sample

    n_mod = 1 + bs
    mod_rows = -(-n_mod // 8) * 8
    c_all = jnp.zeros((mod_rows, d), F32).at[0].set(c_ctx).at[1:n_mod].set(c)

    new_k, new_v, new_f, new_b = [], [], [], []
    for l in range(depth):
        lam_init = 0.8 - 0.6 * math.exp(-0.3 * l)
        mod_all = _modulation(c_all, w_mod[l], b_mod[l]).reshape(mod_rows, 6, d)
        wi = w_in[l]
        o = 0
        pieces = {}
        for name, width in zip(("qa", "ka", "va", "ra", "af", "ab", "qd", "kd", "vd", "ga", "gb"),
                               (GLA_QK, GLA_QK, GLA_VW, GLA_VW, GLA_RANK, GLA_RANK, d, d, d, d, d)):
            pieces[name] = wi[:, o:o + width]
            o += width
        w_main = jnp.concatenate([pieces[k] for k in ("qa", "ka", "va", "ra", "qd", "kd", "vd", "ga", "gb")],
                                 axis=1).astype(BF16)
        w_a = jnp.concatenate([pieces["af"], pieces["ab"], jnp.zeros((d, LANE - 2 * GLA_RANK), F32)],
                              axis=1).astype(BF16)
        w_alpha_pad = jnp.zeros((2, LANE, GLA_QK), F32)
        w_alpha_pad = w_alpha_pad.at[0, 0:GLA_RANK].set(w_alpha_f[l]).at[1, GLA_RANK:2 * GLA_RANK].set(w_alpha_b[l])
        lw = dict(
            w_main=w_main, w_a=w_a, w_alpha_pad=w_alpha_pad.astype(BF16),
            b_alpha=jnp.stack([b_alpha_f[l], b_alpha_b[l]])[:, None, :],
            lam_vecs=jnp.stack([lambda_q1[l], lambda_k1[l], lambda_q2[l], lambda_k2[l]]),
            subln_g=subln_g[l][None, :], gla_norm_g=gla_norm_g[l][None, :],
            w_proj_a=w_proj_a[l].astype(BF16), w_proj_b=w_proj_b[l].astype(BF16), w_out=w_out[l].astype(BF16),
            ln1_g=ln1_g[l][None, :], ln1_b=ln1_b[l][None, :], w_router_t=w_router[l].T,
            w_exp_gate=w_exp_gate[l], w_exp_up=w_exp_up[l], w_exp_down=w_exp_down[l],
            ln2_g=ln2_g[l][None, :], ln2_b=ln2_b[l][None, :])

        zeros = jnp.zeros((2, bp, GLA_HEADS, GLA_DK, GLA_DV), F32)
        gp = _mix_group(xp, mod_all, lambda i, tm: 0, lw, lam_init, alpha, zeros, None)
        y_p = gp["y"]
        new_k.append(y_p[:, COL_KD:COL_KD + d].reshape(bp, sp, DIFF_HEADS, 2, DIFF_DH))
        new_v.append(y_p[:, COL_VD:COL_VD + d].reshape(bp, sp, DIFF_HEADS, 2 * DIFF_DH))
        new_f.append(gp["states"][0])
        new_b.append(gp["states"][1])

        s0 = (state_gla_fwd[:, l], state_gla_bwd[:, l])
        past = cache_diff_k.shape[2]
        ctx = (cache_diff_k[:, l].reshape(bs * past, d), cache_diff_v[:, l].reshape(bs * past, d))
        gs = _mix_group(xs, mod_all, lambda i, tm: 1 + (i * tm) // ss, lw, lam_init, alpha, s0, ctx)
        xp, xs = _ffn_groups(gp, gs, mod_all, lw, alpha)

    return (xp, xs, jnp.stack(new_k, axis=1), jnp.stack(new_v, axis=1),
            jnp.stack(new_f, axis=1), jnp.stack(new_b, axis=1))
```

```python
import functools
import math

import jax
import jax.numpy as jnp
from jax import lax
from jax.experimental import pallas as pl
from jax.experimental.pallas import tpu as pltpu

F32 = jnp.float32
BF16 = jnp.bfloat16
HIGHEST = lax.Precision.HIGHEST

LANE = 128
EPS = 1e-6
GRID_W = 64
GLA_HEADS = 4
GLA_DK = 128
GLA_DV = 256
GLA_QK = GLA_HEADS * GLA_DK
GLA_VW = GLA_HEADS * GLA_DV
GLA_RANK = 16
GLA_TAU = 16.0
CHUNK = 64
DIFF_HEADS = 8
DIFF_DH = 64
ROPE_BASE = 10000.0
ROPE_AXIS_DIM = DIFF_DH // 2
ROPE_P = ROPE_AXIS_DIM // 2
N_EXPERTS = 16
CAPACITY_FACTOR = 2
VMEM_LIMIT = 56 * 1024 * 1024
COMBINE_TM = 256
WIN = 64

COL_QA, COL_KA, COL_VA, COL_RA, COL_QD, COL_KD, COL_VD, COL_GA, COL_GB = (
    0, 512, 1024, 2048, 3072, 4096, 5120, 6144, 7168)
MAIN_W = 8192


def _cparams(sem):
    return pltpu.CompilerParams(dimension_semantics=sem, vmem_limit_bytes=VMEM_LIMIT)


def _ln_plain(x):
    mu = jnp.mean(x, axis=-1, keepdims=True)
    xc = x - mu
    var = jnp.mean(xc * xc, axis=-1, keepdims=True)
    return xc * lax.rsqrt(var + EPS)


def _sigmoid(x):
    return 1.0 / (1.0 + jnp.exp(-x))


def _silu(x):
    return x * _sigmoid(x)


def _mod_kernel(c_ref, w_ref, b_ref, o_ref):
    c = c_ref[...]
    o_ref[...] = jnp.dot(_silu(c), w_ref[...], precision=HIGHEST, preferred_element_type=F32) + b_ref[...]


def _modulation(c_all, w_mod, b_mod):
    rows, d = c_all.shape
    n = w_mod.shape[1]
    tn = 1024
    return pl.pallas_call(
        _mod_kernel,
        out_shape=jax.ShapeDtypeStruct((rows, n), F32),
        grid=(n // tn,),
        in_specs=[pl.BlockSpec((rows, d), lambda j: (0, 0)),
                  pl.BlockSpec((d, tn), lambda j: (0, j)),
                  pl.BlockSpec((1, tn), lambda j: (0, j))],
        out_specs=pl.BlockSpec((rows, tn), lambda j: (0, j)),
        compiler_params=_cparams(("parallel",)),
        name="modulation",
    )(c_all, w_mod, b_mod.reshape(1, n))


def _in_proj_kernel(x_ref, mod_ref, w_ref, wa_ref, y_ref, a_ref, h_scr):
    @pl.when(pl.program_id(1) == 0)
    def _():
        mod = mod_ref[0]
        h = _ln_plain(x_ref[...]) * (1.0 + mod[1:2]) + mod[0:1]
        hb = h.astype(BF16)
        h_scr[...] = hb
        a_ref[...] = jnp.dot(hb, wa_ref[...], preferred_element_type=F32)

    y_ref[...] = jnp.dot(h_scr[...], w_ref[...], preferred_element_type=F32)


def _in_proj(x, mod_all, mod_row, w_main, w_a, tm, tn):
    n, d = x.shape
    nw = w_main.shape[1]
    return pl.pallas_call(
        _in_proj_kernel,
        out_shape=(jax.ShapeDtypeStruct((n, nw), F32), jax.ShapeDtypeStruct((n, LANE), F32)),
        grid=(n // tm, nw // tn),
        in_specs=[pl.BlockSpec((tm, d), lambda i, j: (i, 0)),
                  pl.BlockSpec((1, 6, d), lambda i, j: (mod_row(i, tm), 0, 0)),
                  pl.BlockSpec((d, tn), lambda i, j: (0, j)),
                  pl.BlockSpec((d, LANE), lambda i, j: (0, 0))],
        out_specs=(pl.BlockSpec((tm, tn), lambda i, j: (i, j)),
                   pl.BlockSpec((tm, LANE), lambda i, j: (i, 0))),
        scratch_shapes=[pltpu.VMEM((tm, d), BF16)],
        compiler_params=_cparams(("parallel", "arbitrary")),
        name="in_proj",
    )(x, mod_all, w_main, w_a)


def _gla_kernel(a_ref, q_ref, k_ref, v_ref, wal_ref, bal_ref, s0_ref, o_ref, sout_ref, s_scr, *, rev, nch):
    step = pl.program_id(1)

    @pl.when(step == 0)
    def _():
        s_scr[...] = s0_ref[0]

    z = jnp.dot(a_ref[...].astype(BF16), wal_ref[...], preferred_element_type=F32) + bal_ref[...]
    g = (jnp.minimum(z, 0.0) - jnp.log(1.0 + jnp.exp(-jnp.abs(z)))) * (1.0 / GLA_TAU)
    row = lax.broadcasted_iota(jnp.int32, (CHUNK, CHUNK), 0)
    col = lax.broadcasted_iota(jnp.int32, (CHUNK, CHUNK), 1)
    keep = (col >= row) if rev else (col <= row)
    tri = keep.astype(F32)
    chunks = list(reversed(range(nch)) if rev else range(nch))
    heads = [(slice(h * GLA_DK, (h + 1) * GLA_DK), slice(h * GLA_DV, (h + 1) * GLA_DV)) for h in range(GLA_HEADS)]
    pre = {}
    for ci in chunks:
        rs = slice(ci * CHUNK, (ci + 1) * CHUNK)
        gc = g[rs]
        b = jnp.dot(tri, gc, precision=HIGHEST, preferred_element_type=F32)
        b_last = jnp.sum(gc, axis=0, keepdims=True)
        eb_last_t = jnp.transpose(jnp.broadcast_to(jnp.exp(b_last), (8, GLA_QK)))
        k = k_ref[rs, :]
        pre[ci] = dict(rs=rs, eb_last_t=eb_last_t,
                       qe=(q_ref[rs, :] * (GLA_DK ** -0.5) * jnp.exp(b)).astype(BF16),
                       ke=(k * jnp.exp(-b)).astype(BF16),
                       kd=(k * jnp.exp(b_last - b)).astype(BF16),
                       v=v_ref[rs, :].astype(BF16))
    for ci in chunks:
        c = pre[ci]
        c["a"] = [jnp.where(keep, lax.dot_general(c["qe"][:, ks], c["ke"][:, ks], (((1,), (1,)), ((), ())),
                                                  preferred_element_type=F32), 0.0).astype(BF16)
                  for ks, _ in heads]
    for ci in chunks:
        c = pre[ci]
        c["o"] = [jnp.dot(c["a"][h], c["v"][:, vs], preferred_element_type=F32) for h, (_, vs) in enumerate(heads)]
        c["upd"] = [lax.dot_general(c["kd"][:, ks], c["v"][:, vs], (((0,), (0,)), ((), ())),
                                    preferred_element_type=F32) for ks, vs in heads]
    state = [s_scr[h] for h in range(GLA_HEADS)]
    for ci in chunks:
        c = pre[ci]
        for h, (ks, vs) in enumerate(heads):
            o_ref[c["rs"], vs] = c["o"][h] + jnp.dot(c["qe"][:, ks], state[h].astype(BF16),
                                                     preferred_element_type=F32)
            state[h] = c["eb_last_t"][ks, 0:1] * state[h] + c["upd"][h]
    for h in range(GLA_HEADS):
        s_scr[h] = state[h]

    @pl.when(step == pl.num_programs(1) - 1)
    def _():
        sout_ref[0] = s_scr[...]


def _gla(y, a, w_alpha_pad, b_alpha, s0, batch, seq, rev):
    n = y.shape[0]
    nch = min(seq // CHUNK, 8)
    rows = nch * CHUNK
    ns = seq // rows

    def rowblk(b, s):
        return b * ns + (ns - 1 - s if rev else s)

    state_spec = pl.BlockSpec((1, GLA_HEADS, GLA_DK, GLA_DV), lambda b, s: (b, 0, 0, 0))
    return pl.pallas_call(
        functools.partial(_gla_kernel, rev=rev, nch=nch),
        out_shape=(jax.ShapeDtypeStruct((n, GLA_VW), F32),
                   jax.ShapeDtypeStruct((batch, GLA_HEADS, GLA_DK, GLA_DV), F32)),
        grid=(batch, ns),
        in_specs=[pl.BlockSpec((rows, LANE), lambda b, s: (rowblk(b, s), 0)),
                  pl.BlockSpec((rows, GLA_QK), lambda b, s: (rowblk(b, s), COL_QA // GLA_QK)),
                  pl.BlockSpec((rows, GLA_QK), lambda b, s: (rowblk(b, s), COL_KA // GLA_QK)),
                  pl.BlockSpec((rows, GLA_VW), lambda b, s: (rowblk(b, s), COL_VA // GLA_VW)),
                  pl.BlockSpec((LANE, GLA_QK), lambda b, s: (0, 0)),
                  pl.BlockSpec((1, GLA_QK), lambda b, s: (0, 0)),
                  state_spec],
        out_specs=(pl.BlockSpec((rows, GLA_VW), lambda b, s: (rowblk(b, s), 0)), state_spec),
        scratch_shapes=[pltpu.VMEM((GLA_HEADS, GLA_DK, GLA_DV), F32)],
        compiler_params=_cparams(("parallel", "arbitrary")),
        name="gla_bwd" if rev else "gla_fwd",
    )(a, y, y, y, w_alpha_pad, b_alpha, s0)


def _rope(x, cos_t, sin_a, sin_b):
    return x * cos_t + pltpu.roll(x, LANE - ROPE_P, 1) * sin_a + pltpu.roll(x, ROPE_P, 1) * sin_b


def _lambda(lam_ref, lam_init):
    lq1, lk1, lq2, lk2 = lam_ref[0:1], lam_ref[1:2], lam_ref[2:3], lam_ref[3:4]
    return (jnp.exp(jnp.sum(lq1 * lk1, axis=-1, keepdims=True))
            - jnp.exp(jnp.sum(lq2 * lk2, axis=-1, keepdims=True)) + lam_init)


ATTN_SUB = 128


def _attn_core(q, k, v, lam, g_ref, o_ref, lam_init):
    tq = q.shape[0]
    sub = min(tq, ATTN_SUB)
    lane = lax.broadcasted_iota(jnp.int32, (sub, LANE), 1)
    blocks = [slice(i * sub, (i + 1) * sub) for i in range(tq // sub)]
    s2s = []
    for rows in blocks:
        qs = q[rows] * (DIFF_DH ** -0.5 * math.log2(math.e))
        q2 = jnp.concatenate([jnp.where(lane < DIFF_DH, qs, 0.0), jnp.where(lane >= DIFF_DH, qs, 0.0)],
                             axis=0).astype(BF16)
        s2s.append(lax.dot_general(q2, k, (((1,), (1,)), ((), ())), preferred_element_type=F32))
    ps = [jnp.exp2(s2 - jnp.max(s2, axis=-1, keepdims=True)) for s2 in s2s]
    rs = [1.0 / jnp.sum(p, axis=-1, keepdims=True) for p in ps]
    pvs = [jnp.dot(p.astype(BF16), v, preferred_element_type=F32) for p in ps]
    for rows, pv, r in zip(blocks, pvs, rs):
        pv = pv * r
        o = pv[:sub] - lam * pv[sub:]
        ms = jnp.mean(o * o, axis=-1, keepdims=True)
        o_ref[rows, :] = (o * lax.rsqrt(ms + EPS) * g_ref[...] * (1.0 - lam_init)).astype(o_ref.dtype)


def _attn_ctx_kernel(q_ref, k_ref, v_ref, lam_ref, g_ref, o_ref, *, lam_init):
    lam = _lambda(lam_ref, lam_init)
    _attn_core(q_ref[...], k_ref[...].astype(BF16), v_ref[...].astype(BF16), lam, g_ref, o_ref, lam_init)


def _attn_lat_kernel(q_ref, k_ref, v_ref, kc_ref, vc_ref, cq_ref, saq_ref, sbq_ref, ck_ref, sak_ref, sbk_ref,
                     lam_ref, g_ref, o_ref, k_scr, v_scr, *, seq, lam_init):
    @pl.when(pl.program_id(2) == 0)
    def _():
        k_scr[0:seq] = _rope(k_ref[...], ck_ref[...], sak_ref[...], sbk_ref[...]).astype(BF16)
        k_scr[seq:] = kc_ref[...].astype(BF16)
        v_scr[0:seq] = v_ref[...].astype(BF16)
        v_scr[seq:] = vc_ref[...].astype(BF16)

    lam = _lambda(lam_ref, lam_init)
    q = _rope(q_ref[...], cq_ref[...], saq_ref[...], sbq_ref[...])
    _attn_core(q, k_scr[...], v_scr[...], lam, g_ref, o_ref, lam_init)


def _diff_attn_ctx(y, lam_vecs, subln_g, batch, seq, lam_init):
    n = y.shape[0]
    tq = seq
    hb = lambda col: col // LANE
    return pl.pallas_call(
        functools.partial(_attn_ctx_kernel, lam_init=lam_init),
        out_shape=jax.ShapeDtypeStruct((n, DIFF_HEADS * LANE), BF16),
        grid=(batch, DIFF_HEADS),
        in_specs=[pl.BlockSpec((tq, LANE), lambda b, h: (b, hb(COL_QD) + h)),
                  pl.BlockSpec((seq, LANE), lambda b, h: (b, hb(COL_KD) + h)),
                  pl.BlockSpec((seq, LANE), lambda b, h: (b, hb(COL_VD) + h)),
                  pl.BlockSpec((4, DIFF_DH), lambda b, h: (0, 0)),
                  pl.BlockSpec((1, LANE), lambda b, h: (0, 0))],
        out_specs=pl.BlockSpec((tq, LANE), lambda b, h: (b, h)),
        compiler_params=_cparams(("parallel", "parallel")),
        name="diff_attn_ctx",
    )(y, y, y, lam_vecs, subln_g)


def _diff_attn_lat(y, k_cache, v_cache, tabs, lam_vecs, subln_g, batch, seq, past, tq, lam_init):
    n = y.shape[0]
    nq = seq // tq
    hb = lambda col: col // LANE
    cos_t, sin_a, sin_b = tabs
    qtab = pl.BlockSpec((tq, LANE), lambda b, h, i: (i, 0))
    ktab = pl.BlockSpec((seq, LANE), lambda b, h, i: (0, 0))
    return pl.pallas_call(
        functools.partial(_attn_lat_kernel, seq=seq, lam_init=lam_init),
        out_shape=jax.ShapeDtypeStruct((n, DIFF_HEADS * LANE), BF16),
        grid=(batch, DIFF_HEADS, nq),
        in_specs=[pl.BlockSpec((tq, LANE), lambda b, h, i: (b * nq + i, hb(COL_QD) + h)),
                  pl.BlockSpec((seq, LANE), lambda b, h, i: (b, hb(COL_KD) + h)),
                  pl.BlockSpec((seq, LANE), lambda b, h, i: (b, hb(COL_VD) + h)),
                  pl.BlockSpec((past, LANE), lambda b, h, i: (b, h)),
                  pl.BlockSpec((past, LANE), lambda b, h, i: (b, h)),
                  qtab, qtab, qtab, ktab, ktab, ktab,
                  pl.BlockSpec((4, DIFF_DH), lambda b, h, i: (0, 0)),
                  pl.BlockSpec((1, LANE), lambda b, h, i: (0, 0))],
        out_specs=pl.BlockSpec((tq, LANE), lambda b, h, i: (b * nq + i, h)),
        scratch_shapes=[pltpu.VMEM((seq + past, LANE), BF16), pltpu.VMEM((seq + past, LANE), BF16)],
        compiler_params=_cparams(("parallel", "parallel", "arbitrary")),
        name="diff_attn_lat",
    )(y, y, y, k_cache, v_cache, cos_t, sin_a, sin_b, cos_t, sin_a, sin_b, lam_vecs, subln_g)


def _rope_tables(seq):
    rows = seq // GRID_W
    pos_r = jnp.repeat(jnp.arange(rows, dtype=F32), GRID_W)
    pos_c = jnp.tile(jnp.arange(GRID_W, dtype=F32), rows)
    inv = ROPE_BASE ** (-jnp.arange(ROPE_P, dtype=F32) / ROPE_P)
    ang_r = pos_r[:, None] * inv
    ang_c = pos_c[:, None] * inv
    zeros = jnp.zeros((seq, ROPE_P), F32)

    def axis_tabs(ang):
        cs, sn = jnp.cos(ang), jnp.sin(ang)
        return (jnp.concatenate([cs, cs], -1), jnp.concatenate([-sn, zeros], -1), jnp.concatenate([zeros, sn], -1))

    tr, tc = axis_tabs(ang_r), axis_tabs(ang_c)
    reps = LANE // (2 * ROPE_AXIS_DIM)
    return tuple(jnp.tile(jnp.concatenate([a, b], -1), (1, reps)) for a, b in zip(tr, tc))


MERGE_SUB = 128


def _merge_kernel(of_ref, ob_ref, ra_ref, od_ref, ga_ref, gb_ref, x_ref, mod_ref, gn_ref, wpa_ref, wpb_ref,
                  wout_ref, l1g_ref, l1b_ref, wr_ref, x1_ref, h2_ref, aff_ref, *, alpha):
    mod = mod_ref[0]
    tm = x_ref.shape[0]
    sub = min(tm, MERGE_SUB)
    blocks = [slice(i * sub, (i + 1) * sub) for i in range(tm // sub)]

    def gla_out(rows):
        o = of_ref[rows, :] + ob_ref[rows, :]
        parts = []
        for h in range(GLA_HEADS):
            oh = o[:, h * GLA_DV:(h + 1) * GLA_DV]
            parts.append(oh * lax.rsqrt(jnp.mean(oh * oh, axis=-1, keepdims=True) + EPS) * gn_ref[...])
        return (jnp.concatenate(parts, axis=-1) * _silu(ra_ref[rows, :])).astype(BF16)

    o_as = [gla_out(rows) for rows in blocks]
    pas = [jnp.dot(o_a, wpa_ref[...], preferred_element_type=F32) for o_a in o_as]
    pbs = [jnp.dot(od_ref[rows, :], wpb_ref[...], preferred_element_type=F32) for rows in blocks]
    mergeds = [(_sigmoid(ga_ref[rows, :]) * pa + _sigmoid(gb_ref[rows, :]) * pb).astype(BF16)
               for rows, pa, pb in zip(blocks, pas, pbs)]
    mixes = [jnp.dot(merged, wout_ref[...], preferred_element_type=F32) for merged in mergeds]
    for rows, mix in zip(blocks, mixes):
        x1 = _ln_plain(alpha * x_ref[rows, :] + mod[2:3] * mix) * l1g_ref[...] + l1b_ref[...]
        x1_ref[rows, :] = x1
        h2 = _ln_plain(x1) * (1.0 + mod[4:5]) + mod[3:4]
        h2_ref[rows, :] = h2.astype(h2_ref.dtype)
        logits = lax.dot_general(wr_ref[...], h2, (((1,), (1,)), ((), ())), precision=HIGHEST,
                                 preferred_element_type=F32)
        e = jnp.exp(logits - jnp.max(logits, axis=0, keepdims=True))
        aff_ref[:, rows] = e / jnp.sum(e, axis=0, keepdims=True)


def _merge(o_f, o_b, y, o_d, x, mod_all, mod_row, gn, wpa, wpb, wout, l1g, l1b, wr_t, tm, alpha):
    n, d = x.shape
    cb = lambda col: col // d
    rowspec = lambda col: pl.BlockSpec((tm, d), lambda i: (i, col))
    const = lambda shape: pl.BlockSpec(shape, lambda i: (0,) * len(shape))
    return pl.pallas_call(
        functools.partial(_merge_kernel, alpha=alpha),
        out_shape=(jax.ShapeDtypeStruct((n, d), F32), jax.ShapeDtypeStruct((n, d), F32),
                   jax.ShapeDtypeStruct((N_EXPERTS, n), F32)),
        grid=(n // tm,),
        in_specs=[rowspec(0), rowspec(0),
                  rowspec(cb(COL_RA)), rowspec(0), rowspec(cb(COL_GA)), rowspec(cb(COL_GB)), rowspec(0),
                  pl.BlockSpec((1, 6, d), lambda i: (mod_row(i, tm), 0, 0)),
                  const((1, GLA_DV)), const((d, d)), const((d, d)), const((d, d)),
                  const((1, d)), const((1, d)), const((N_EXPERTS, d))],
        out_specs=(pl.BlockSpec((tm, d), lambda i: (i, 0)), pl.BlockSpec((tm, d), lambda i: (i, 0)),
                   pl.BlockSpec((N_EXPERTS, tm), lambda i: (0, i))),
        compiler_params=_cparams(("parallel",)),
        name="merge",
    )(o_f, o_b, y, o_d, y, y, x, mod_all, gn, wpa, wpb, wout, l1g, l1b, wr_t)


TM_ID, TM_ONE, TM_GHI, TM_GMID, TM_GLO, TM_POS = 0, 16, 32, 48, 64, 80
EXP_BISECT = 7
LIN_BISECT = 50


def _route_kernel(aff_ref, bind_ref, tm_ref, pos_ref, offb_ref, *, cap):
    n_exp, n = aff_ref.shape
    nb = n // LANE
    aff = aff_ref[...]
    capf = float(cap)

    def count_ge(thr):
        return jnp.sum(jnp.where(aff >= thr, 1.0, 0.0), axis=1, keepdims=True)

    def exp_step(_, c):
        klo, khi = c
        kmid = jnp.floor((klo + khi) * 0.5)
        ok = count_ge(jnp.exp2(kmid)) >= capf
        return jnp.where(ok, kmid, klo), jnp.where(ok, khi, kmid)

    klo, khi = lax.fori_loop(0, EXP_BISECT, exp_step,
                             (jnp.full((n_exp, 1), -127.0, F32), jnp.full((n_exp, 1), 1.0, F32)))
    lo0 = jnp.where(klo <= -127.0, 0.0, jnp.exp2(klo))
    hi0 = jnp.exp2(khi)

    def lin_step(_, c):
        lo, hi = c
        mid = lo + (hi - lo) * 0.5
        ok = count_ge(mid) >= capf
        return jnp.where(ok, mid, lo), jnp.where(ok, hi, mid)

    lo, hi = lax.fori_loop(0, LIN_BISECT, lin_step, (lo0, hi0))

    row = lax.broadcasted_iota(jnp.int32, (LANE, LANE), 0)
    col = lax.broadcasted_iota(jnp.int32, (LANE, LANE), 1)
    incl = jnp.where(row <= col, 1.0, 0.0).astype(BF16)
    excl = jnp.where(row < col, 1.0, 0.0).astype(BF16)
    bind = bind_ref[...]

    def prefix(mask):
        xb = jnp.where(mask, 1.0, 0.0).astype(BF16)
        stacked = jnp.concatenate([xb[:, j * LANE:(j + 1) * LANE] for j in range(nb)], axis=0)
        within = jnp.dot(stacked, incl, preferred_element_type=F32)
        within = jnp.concatenate([within[j * n_exp:(j + 1) * n_exp, :] for j in range(nb)], axis=1)
        block_tot = jnp.dot(xb, bind, preferred_element_type=F32)
        block_off = jnp.dot(block_tot.astype(BF16), excl, preferred_element_type=F32)
        off_hi = jnp.floor(block_off * (1.0 / LANE))
        off_lo = block_off - off_hi * LANE
        nt_dims = (((1,), (1,)), ((), ()))
        spread = (lax.dot_general(off_hi.astype(BF16), bind, nt_dims, preferred_element_type=F32) * LANE
                  + lax.dot_general(off_lo.astype(BF16), bind, nt_dims, preferred_element_type=F32))
        return within + spread, block_off

    above = aff >= hi
    tied = jnp.logical_and(aff >= lo, jnp.logical_not(above))
    need = capf - jnp.sum(jnp.where(above, 1.0, 0.0), axis=1, keepdims=True)
    tie_rank, _ = prefix(tied)
    sel = jnp.logical_or(above, jnp.logical_and(tied, tie_rank <= need))
    cum, block_off = prefix(sel)
    pos = jnp.where(sel, cum - 1.0, -1.0)
    pos_ref[...] = pos
    offb_ref[...] = block_off

    g_hi = aff.astype(BF16).astype(F32)
    rem = aff - g_hi
    g_mid = rem.astype(BF16).astype(F32)
    g_lo = (rem - g_mid).astype(BF16).astype(F32)
    lane_id = lax.broadcasted_iota(jnp.int32, (n_exp, LANE), 1).astype(F32)
    ones = jnp.ones((n_exp, LANE), F32)
    fill = jnp.zeros((LANE - 6 * n_exp, LANE), F32)
    for j in range(nb):
        cs = slice(j * LANE, (j + 1) * LANE)
        local = lane_id + float(LANE * (j % (COMBINE_TM // LANE)))
        fields = jnp.concatenate([local, ones, g_hi[:, cs], g_mid[:, cs], g_lo[:, cs], pos[:, cs], fill], axis=0)
        tm_ref[cs, :] = jnp.transpose(fields)


def _route(aff_t, cap):
    n_exp, n = aff_t.shape
    bind = (jnp.arange(n, dtype=jnp.int32)[:, None] // LANE == jnp.arange(LANE, dtype=jnp.int32)[None, :]).astype(BF16)
    return pl.pallas_call(
        functools.partial(_route_kernel, cap=cap),
        out_shape=(jax.ShapeDtypeStruct((n, LANE), F32), jax.ShapeDtypeStruct((n_exp, n), F32),
                   jax.ShapeDtypeStruct((n_exp, LANE), F32)),
        compiler_params=pltpu.CompilerParams(vmem_limit_bytes=VMEM_LIMIT),
        name="route",
    )(aff_t, bind)


def _compact_kernel(base_ref, npass_ref, pos_ref, tm_ref, acc_ref, *, cap):
    t = pl.program_id(0)
    nt = pl.num_programs(0)
    n_exp = pos_ref.shape[0]

    @pl.when(t == 0)
    def _():
        acc_ref[...] = jnp.zeros(acc_ref.shape, F32)

    pos = pos_ref[...]
    table = tm_ref[...].astype(BF16)
    lane = lax.broadcasted_iota(jnp.int32, (1, LANE), 1)
    tile_scale = jnp.where(jnp.logical_and(lane >= TM_ONE, lane < TM_GHI), t.astype(F32), 1.0)
    slot_iota = lax.broadcasted_iota(jnp.int32, (WIN, pos.shape[1]), 0).astype(F32)

    def one_pass(p, carry):
        onehot = jnp.concatenate(
            [jnp.where(pos[e:e + 1, :] - (base_ref[e * nt + t] + p * WIN).astype(F32) == slot_iota, 1.0, 0.0)
             for e in range(n_exp)], axis=0).astype(BF16)
        rec = jnp.dot(onehot, table, preferred_element_type=F32) * tile_scale
        for e in range(n_exp):
            start = pl.multiple_of(jnp.minimum(base_ref[e * nt + t] + p * WIN, cap), 8)
            mine = jnp.logical_and(lane % n_exp == e, lane < TM_POS)
            acc_ref[pl.ds(start, WIN), :] += jnp.where(mine, rec[e * WIN:(e + 1) * WIN, :], 0.0)
        return carry

    lax.fori_loop(0, npass_ref[t], one_pass, 0)


def _compact(pos, tm, base, npass, cap):
    n_exp, n = pos.shape
    tm_rows = COMBINE_TM
    grid_spec = pltpu.PrefetchScalarGridSpec(
        num_scalar_prefetch=2,
        grid=(n // tm_rows,),
        in_specs=[pl.BlockSpec((n_exp, tm_rows), lambda i, *_: (0, i)),
                  pl.BlockSpec((tm_rows, LANE), lambda i, *_: (i, 0))],
        out_specs=pl.BlockSpec((cap + WIN, LANE), lambda i, *_: (0, 0)))
    return pl.pallas_call(
        functools.partial(_compact_kernel, cap=cap),
        out_shape=jax.ShapeDtypeStruct((cap + WIN, LANE), F32),
        grid_spec=grid_spec,
        compiler_params=_cparams(("arbitrary",)),
        name="compact",
    )(base.reshape(-1), npass, pos, tm)


def _routing(aff_t, cap):
    n_exp, n = aff_t.shape
    tm, pos, offb = _route(aff_t, cap)
    blocks_per_tile = COMBINE_TM // LANE
    off = offb[:, :n // LANE].astype(jnp.int32)[:, ::blocks_per_tile]
    nxt = jnp.concatenate([off[:, 1:], jnp.full((n_exp, 1), cap, jnp.int32)], axis=1)
    base = (off // 8) * 8
    npass = jnp.maximum(jnp.max(-(-(nxt - base) // WIN), axis=0), 1).astype(jnp.int32)
    acc = _compact(pos, tm, base, npass, cap)[:cap]
    idx = (acc[:, TM_ID:TM_ID + n_exp] + COMBINE_TM * acc[:, TM_ONE:TM_ONE + n_exp]).T.astype(jnp.int32)
    gate = (acc[:, TM_GHI:TM_GHI + n_exp] + acc[:, TM_GMID:TM_GMID + n_exp] + acc[:, TM_GLO:TM_GLO + n_exp]).T
    return idx, gate, tm, base, npass


def _moe_kernel(idx_ref, hp_hbm, hs_hbm, gate_ref, wg_ref, wu_ref, wd_ref, yp_ref, ys_ref, xf_scr, xb_scr, sem,
                *, rows, cap_p):
    e = pl.program_id(0)
    f = pl.program_id(1)
    nf = pl.num_programs(1)
    cap = xf_scr.shape[0]

    @pl.when(f == 0)
    def _():
        def issue_from(h_hbm):
            def issue(r, carry):
                tok = idx_ref[e * cap + r]
                pltpu.make_async_copy(h_hbm.at[pl.ds(tok, 1)], xf_scr.at[pl.ds(r, 1)], sem).start()
                return carry
            return issue

        lax.fori_loop(0, cap_p, issue_from(hp_hbm), 0, unroll=8)
        lax.fori_loop(cap_p, cap, issue_from(hs_hbm), 0, unroll=8)

        def drain(r, carry):
            pltpu.make_async_copy(hp_hbm.at[pl.ds(0, 1)], xf_scr.at[pl.ds(r, 1)], sem).wait()
            return carry

        lax.fori_loop(0, cap, drain, 0, unroll=8)
        xb_scr[...] = xf_scr[...].astype(BF16)

    wg = wg_ref[0].astype(BF16)
    wu = wu_ref[0].astype(BF16)
    wd = wd_ref[0].astype(BF16)
    for t in range(cap // rows):
        lo = t * rows
        out_ref, olo = (yp_ref, lo) if lo < cap_p else (ys_ref, lo - cap_p)
        xr = xb_scr[lo:lo + rows, :]
        hg = jnp.dot(xr, wg, preferred_element_type=F32)
        hu = jnp.dot(xr, wu, preferred_element_type=F32)
        hid = (_silu(hg) * hu).astype(BF16)
        prev = jnp.where(f > 0, out_ref[0, olo:olo + rows, :], 0.0)
        out_ref[0, olo:olo + rows, :] = prev + jnp.dot(hid, wd, preferred_element_type=F32)

    @pl.when(f == nf - 1)
    def _():
        cap_s = cap - cap_p
        yp_ref[0, 0:cap_p, :] = yp_ref[0, 0:cap_p, :] * gate_ref[0, 0:cap_p, :]
        ys_ref[0, 0:cap_s, :] = ys_ref[0, 0:cap_s, :] * gate_ref[0, cap_p:, :]
        yp_ref[0, cap_p:, :] = jnp.zeros((WIN, yp_ref.shape[2]), F32)
        ys_ref[0, cap_s:, :] = jnp.zeros((WIN, ys_ref.shape[2]), F32)


def _moe(idx, hp, hs, gate, wg, wu, wd, cap_p, ft, rows):
    e, cap = idx.shape
    cap_s = cap - cap_p
    d = hp.shape[1]
    ff = wg.shape[2]
    grid_spec = pltpu.PrefetchScalarGridSpec(
        num_scalar_prefetch=1,
        grid=(e, ff // ft),
        in_specs=[pl.BlockSpec(memory_space=pl.ANY),
                  pl.BlockSpec(memory_space=pl.ANY),
                  pl.BlockSpec((1, cap, 1), lambda i, f, idx_ref: (i, 0, 0)),
                  pl.BlockSpec((1, d, ft), lambda i, f, idx_ref: (i, 0, f)),
                  pl.BlockSpec((1, d, ft), lambda i, f, idx_ref: (i, 0, f)),
                  pl.BlockSpec((1, ft, d), lambda i, f, idx_ref: (i, f, 0))],
        out_specs=(pl.BlockSpec((1, cap_p + WIN, d), lambda i, f, idx_ref: (i, 0, 0)),
                   pl.BlockSpec((1, cap_s + WIN, d), lambda i, f, idx_ref: (i, 0, 0))),
        scratch_shapes=[pltpu.VMEM((cap, d), F32), pltpu.VMEM((cap, d), BF16), pltpu.SemaphoreType.DMA])
    return pl.pallas_call(
        functools.partial(_moe_kernel, rows=rows, cap_p=cap_p),
        out_shape=(jax.ShapeDtypeStruct((e, cap_p + WIN, d), F32), jax.ShapeDtypeStruct((e, cap_s + WIN, d), F32)),
        grid_spec=grid_spec,
        compiler_params=_cparams(("arbitrary", "arbitrary")),
        name="moe",
    )(idx.reshape(-1), hp, hs, gate, wg, wu, wd)


def _combine_kernel(base_ref, npass_ref, x1_ref, tm_ref, mod_ref, g_ref, b_ref, ye_hbm, o_ref,
                    ybuf, xbuf, sem, xsem, *, alpha):
    t = pl.program_id(0)
    nt = pl.num_programs(0)
    n_exp = ye_hbm.shape[0]
    tm = x1_ref.shape[0]
    cap = ye_hbm.shape[1] - WIN

    def window(tile, e, p, dst, s):
        start = pl.multiple_of(jnp.minimum(base_ref[e * nt + tile] + p * WIN, cap), 8)
        return pltpu.make_async_copy(ye_hbm.at[e, pl.ds(start, WIN), :], dst.at[pl.ds(e * WIN, WIN), :], s)

    def start_first_pass(tile, slot):
        for e in range(n_exp):
            window(tile, e, 0, ybuf.at[slot], sem.at[slot]).start()

    def expand(yw, p):
        lane = lax.broadcasted_iota(jnp.int32, (tm, LANE), 1).astype(F32)
        groups = []
        for gi in range(n_exp * WIN // LANE):
            ea, eb = 2 * gi, 2 * gi + 1
            ra = tm_ref[:, TM_POS + ea:TM_POS + ea + 1] - (base_ref[ea * nt + t] + p * WIN).astype(F32)
            rb = tm_ref[:, TM_POS + eb:TM_POS + eb + 1] - (base_ref[eb * nt + t] + (p - 1) * WIN).astype(F32)
            groups.append(jnp.where(lane == jnp.where(lane < WIN, ra, rb), 1.0, 0.0).astype(BF16))
        onehot = jnp.concatenate(groups, axis=1)
        hi = yw.astype(BF16)
        lo = (yw - hi.astype(F32)).astype(BF16)
        return (jnp.dot(onehot, hi, preferred_element_type=F32) + jnp.dot(onehot, lo, preferred_element_type=F32))

    @pl.when(t == 0)
    def _():
        start_first_pass(0, 0)

    slot = t % 2
    for e in range(n_exp):
        window(t, e, 0, ybuf.at[slot], sem.at[slot]).wait()

    @pl.when(t + 1 < nt)
    def _():
        start_first_pass(t + 1, 1 - slot)

    acc = expand(ybuf[slot], 0)

    def extra_pass(p, acc):
        for e in range(n_exp):
            window(t, e, p, xbuf, xsem).start()
        for e in range(n_exp):
            window(t, e, p, xbuf, xsem).wait()
        return acc + expand(xbuf[...], p)

    acc = lax.fori_loop(1, npass_ref[t], extra_pass, acc)
    mod = mod_ref[0]
    o_ref[...] = _ln_plain(alpha * x1_ref[...] + mod[5:6] * acc) * g_ref[...] + b_ref[...]


def _combine(x1, table, base, npass, ye, mod_all, mod_row, g, b, alpha):
    n, d = x1.shape
    n_exp = ye.shape[0]
    tm = COMBINE_TM
    grid_spec = pltpu.PrefetchScalarGridSpec(
        num_scalar_prefetch=2,
        grid=(n // tm,),
        in_specs=[pl.BlockSpec((tm, d), lambda i, *_: (i, 0)),
                  pl.BlockSpec((tm, LANE), lambda i, *_: (i, 0)),
                  pl.BlockSpec((1, 6, d), lambda i, *_: (mod_row(i, tm), 0, 0)),
                  pl.BlockSpec((1, d), lambda i, *_: (0, 0)), pl.BlockSpec((1, d), lambda i, *_: (0, 0)),
                  pl.BlockSpec(memory_space=pl.ANY)],
        out_specs=pl.BlockSpec((tm, d), lambda i, *_: (i, 0)),
        scratch_shapes=[pltpu.VMEM((2, n_exp * WIN, d), F32), pltpu.VMEM((n_exp * WIN, d), F32),
                        pltpu.SemaphoreType.DMA((2,)), pltpu.SemaphoreType.DMA])
    return pl.pallas_call(
        functools.partial(_combine_kernel, alpha=alpha),
        out_shape=jax.ShapeDtypeStruct((n, d), F32),
        grid_spec=grid_spec,
        compiler_params=_cparams(("arbitrary",)),
        name="combine",
    )(base.reshape(-1), npass, x1, table, mod_all, g, b, ye)


def _pick(n, pref):
    t = pref
    while n % t:
        t //= 2
    return t


def _mix_group(x3, mod_all, mod_row, lw, lam_init, alpha, s0, ctx):
    batch, seq, d = x3.shape
    n = batch * seq
    x = x3.reshape(n, d)
    tile_span = n if ctx is None else seq
    y, a = _in_proj(x, mod_all, mod_row, lw["w_main"], lw["w_a"], _pick(tile_span, 1024), 1024)
    o_f, s_f = _gla(y, a, lw["w_alpha_pad"][0], lw["b_alpha"][0], s0[0], batch, seq, False)
    o_b, s_b = _gla(y, a, lw["w_alpha_pad"][1], lw["b_alpha"][1], s0[1], batch, seq, True)
    if ctx is None:
        o_d = _diff_attn_ctx(y, lw["lam_vecs"], lw["subln_g"], batch, seq, lam_init)
    else:
        k_cache, v_cache = ctx
        past = k_cache.shape[0] // batch
        o_d = _diff_attn_lat(y, k_cache, v_cache, _rope_tables(seq), lw["lam_vecs"], lw["subln_g"],
                             batch, seq, past, _pick(seq, 1024), lam_init)
    tm = _pick(seq, 256)
    x1, h2, aff_t = _merge(o_f, o_b, y, o_d, x, mod_all, mod_row, lw["gla_norm_g"], lw["w_proj_a"], lw["w_proj_b"],
                           lw["w_out"], lw["ln1_g"], lw["ln1_b"], lw["w_router_t"], tm, alpha)
    idx, gate, table, base, npass = _routing(aff_t, CAPACITY_FACTOR * n // N_EXPERTS)
    return dict(x1=x1, h2=h2, gate=gate, idx=idx, table=table, base=base, npass=npass, y=y, states=(s_f, s_b),
                mod_row=mod_row, shape=(batch, seq, d))


def _ffn_groups(gp, gs, mod_all, lw, alpha):
    cap_p, cap_s = gp["idx"].shape[1], gs["idx"].shape[1]
    idx = jnp.concatenate([gp["idx"], gs["idx"]], axis=1)
    gate = jnp.concatenate([gp["gate"], gs["gate"]], axis=1)[..., None]
    ye_p, ye_s = _moe(idx, gp["h2"], gs["h2"], gate, lw["w_exp_gate"], lw["w_exp_up"], lw["w_exp_down"],
                      cap_p, 256, _pick(math.gcd(cap_p, cap_s), 512))
    outs = []
    for g, ye in ((gp, ye_p), (gs, ye_s)):
        batch, seq, d = g["shape"]
        x2 = _combine(g["x1"], g["table"], g["base"], g["npass"], ye, mod_all, g["mod_row"],
                      lw["ln2_g"], lw["ln2_b"], alpha)
        outs.append(x2.reshape(batch, seq, d))
    return outs


def kernel(x_prompt, x_sample, cache_diff_k, cache_diff_v, state_gla_fwd, state_gla_bwd, c, c_ctx, w_mod, b_mod, w_in, w_alpha_f, b_alpha_f, w_alpha_b, b_alpha_b, gla_norm_g, w_proj_a, lambda_q1, lambda_k1, lambda_q2, lambda_k2, subln_g, w_proj_b, w_out, ln1_g, ln1_b, w_router, w_exp_gate, w_exp_up, w_exp_down, ln2_g, ln2_b):
    depth = w_in.shape[0]
    bp, sp, d = x_prompt.shape
    bs, ss, _ = x_sample.shape
    alpha = (2.0 * depth) ** 0.25
    xp, xs = x_prompt, x_sample

    n_mod = 1 + bs
    mod_rows = -(-n_mod // 8) * 8
    c_all = jnp.zeros((mod_rows, d), F32).at[0].set(c_ctx).at[1:n_mod].set(c)

    new_k, new_v, new_f, new_b = [], [], [], []
    for l in range(depth):
        lam_init = 0.8 - 0.6 * math.exp(-0.3 * l)
        mod_all = _modulation(c_all, w_mod[l], b_mod[l]).reshape(mod_rows, 6, d)
        wi = w_in[l]
        o = 0
        pieces = {}
        for name, width in zip(("qa", "ka", "va", "ra", "af", "ab", "qd", "kd", "vd", "ga", "gb"),
                               (GLA_QK, GLA_QK, GLA_VW, GLA_VW, GLA_RANK, GLA_RANK, d, d, d, d, d)):
            pieces[name] = wi[:, o:o + width]
            o += width
        w_main = jnp.concatenate([pieces[k] for k in ("qa", "ka", "va", "ra", "qd", "kd", "vd", "ga", "gb")],
                                 axis=1).astype(BF16)
        w_a = jnp.concatenate([pieces["af"], pieces["ab"], jnp.zeros((d, LANE - 2 * GLA_RANK), F32)],
                              axis=1).astype(BF16)
        w_alpha_pad = jnp.zeros((2, LANE, GLA_QK), F32)
        w_alpha_pad = w_alpha_pad.at[0, 0:GLA_RANK].set(w_alpha_f[l]).at[1, GLA_RANK:2 * GLA_RANK].set(w_alpha_b[l])
        lw = dict(
            w_main=w_main, w_a=w_a, w_alpha_pad=w_alpha_pad.astype(BF16),
            b_alpha=jnp.stack([b_alpha_f[l], b_alpha_b[l]])[:, None, :],
            lam_vecs=jnp.stack([lambda_q1[l], lambda_k1[l], lambda_q2[l], lambda_k2[l]]),
            subln_g=subln_g[l][None, :], gla_norm_g=gla_norm_g[l][None, :],
            w_proj_a=w_proj_a[l].astype(BF16), w_proj_b=w_proj_b[l].astype(BF16), w_out=w_out[l].astype(BF16),
            ln1_g=ln1_g[l][None, :], ln1_b=ln1_b[l][None, :], w_router_t=w_router[l].T,
            w_exp_gate=w_exp_gate[l], w_exp_up=w_exp_up[l], w_exp_down=w_exp_down[l],
            ln2_g=ln2_g[l][None, :], ln2_b=ln2_b[l][None, :])

        zeros = jnp.zeros((2, bp, GLA_HEADS, GLA_DK, GLA_DV), F32)
        gp = _mix_group(xp, mod_all, lambda i, tm: 0, lw, lam_init, alpha, zeros, None)
        y_p = gp["y"]
        new_k.append(y_p[:, COL_KD:COL_KD + d].reshape(bp, sp, DIFF_HEADS, 2, DIFF_DH))
        new_v.append(y_p[:, COL_VD:COL_VD + d].reshape(bp, sp, DIFF_HEADS, 2 * DIFF_DH))
        new_f.append(gp["states"][0])
        new_b.append(gp["states"][1])

        s0 = (state_gla_fwd[:, l], state_gla_bwd[:, l])
        past = cache_diff_k.shape[2]
        ctx = (cache_diff_k[:, l].reshape(bs * past, d), cache_diff_v[:, l].reshape(bs * past, d))
        gs = _mix_group(xs, mod_all, lambda i, tm: 1 + (i * tm) // ss, lw, lam_init, alpha, s0, ctx)
        xp, xs = _ffn_groups(gp, gs, mod_all, lw, alpha)

    return (xp, xs, jnp.stack(new_k, axis=1), jnp.stack(new_v, axis=1),
            jnp.stack(new_f, axis=1), jnp.stack(new_b, axis=1))
```

```python
import functools
import math

import jax
import jax.numpy as jnp
from jax import lax
from jax.experimental import pallas as pl
from jax.experimental.pallas import tpu as pltpu

F32 = jnp.float32
BF16 = jnp.bfloat16
HIGHEST = lax.Precision.HIGHEST

LANE = 128
EPS = 1e-6
GRID_W = 64
GLA_HEADS = 4
GLA_DK = 128
GLA_DV = 256
GLA_QK = GLA_HEADS * GLA_DK
GLA_VW = GLA_HEADS * GLA_DV
GLA_RANK = 16
GLA_TAU = 16.0
CHUNK = 64
DIFF_HEADS = 8
DIFF_DH = 64
ROPE_BASE = 10000.0
ROPE_AXIS_DIM = DIFF_DH // 2
ROPE_P = ROPE_AXIS_DIM // 2
N_EXPERTS = 16
CAPACITY_FACTOR = 2
VMEM_LIMIT = 56 * 1024 * 1024
COMBINE_TM = 256
WIN = 64

COL_QA, COL_KA, COL_VA, COL_RA, COL_QD, COL_KD, COL_VD, COL_GA, COL_GB = (
    0, 512, 1024, 2048, 3072, 4096, 5120, 6144, 7168)
MAIN_W = 8192


def _cparams(sem):
    return pltpu.CompilerParams(dimension_semantics=sem, vmem_limit_bytes=VMEM_LIMIT)


def _ln_plain(x):
    mu = jnp.mean(x, axis=-1, keepdims=True)
    xc = x - mu
    var = jnp.mean(xc * xc, axis=-1, keepdims=True)
    return xc * lax.rsqrt(var + EPS)


def _sigmoid(x):
    return 1.0 / (1.0 + jnp.exp(-x))


def _silu(x):
    return x * _sigmoid(x)


def _mod_kernel(c_ref, w_ref, b_ref, o_ref):
    c = c_ref[...]
    o_ref[...] = jnp.dot(_silu(c), w_ref[...], precision=HIGHEST, preferred_element_type=F32) + b_ref[...]


def _modulation(c_all, w_mod, b_mod):
    rows, d = c_all.shape
    n = w_mod.shape[1]
    tn = 1024
    return pl.pallas_call(
        _mod_kernel,
        out_shape=jax.ShapeDtypeStruct((rows, n), F32),
        grid=(n // tn,),
        in_specs=[pl.BlockSpec((rows, d), lambda j: (0, 0)),
                  pl.BlockSpec((d, tn), lambda j: (0, j)),
                  pl.BlockSpec((1, tn), lambda j: (0, j))],
        out_specs=pl.BlockSpec((rows, tn), lambda j: (0, j)),
        compiler_params=_cparams(("parallel",)),
        name="modulation",
    )(c_all, w_mod, b_mod.reshape(1, n))


def _in_proj_kernel(x_ref, mod_ref, w_ref, wa_ref, y_ref, a_ref, h_scr):
    @pl.when(pl.program_id(1) == 0)
    def _():
        mod = mod_ref[0]
        h = _ln_plain(x_ref[...]) * (1.0 + mod[1:2]) + mod[0:1]
        hb = h.astype(BF16)
        h_scr[...] = hb
        a_ref[...] = jnp.dot(hb, wa_ref[...], preferred_element_type=F32)

    y_ref[...] = jnp.dot(h_scr[...], w_ref[...], preferred_element_type=F32).astype(y_ref.dtype)


def _in_proj(x, mod_all, mod_row, w_main, w_a, tm, tn, y_dtype):
    n, d = x.shape
    nw = w_main.shape[1]
    return pl.pallas_call(
        _in_proj_kernel,
        out_shape=(jax.ShapeDtypeStruct((n, nw), y_dtype), jax.ShapeDtypeStruct((n, LANE), F32)),
        grid=(n // tm, nw // tn),
        in_specs=[pl.BlockSpec((tm, d), lambda i, j: (i, 0)),
                  pl.BlockSpec((1, 6, d), lambda i, j: (mod_row(i, tm), 0, 0)),
                  pl.BlockSpec((d, tn), lambda i, j: (0, j)),
                  pl.BlockSpec((d, LANE), lambda i, j: (0, 0))],
        out_specs=(pl.BlockSpec((tm, tn), lambda i, j: (i, j)),
                   pl.BlockSpec((tm, LANE), lambda i, j: (i, 0))),
        scratch_shapes=[pltpu.VMEM((tm, d), BF16)],
        compiler_params=_cparams(("parallel", "arbitrary")),
        name="in_proj",
    )(x, mod_all, w_main, w_a)


def _gla_kernel(a_ref, q_ref, k_ref, v_ref, wal_ref, bal_ref, s0_ref, o_ref, sout_ref, s_scr, *, rev, nch):
    step = pl.program_id(1)

    @pl.when(step == 0)
    def _():
        s_scr[...] = s0_ref[0]

    z = jnp.dot(a_ref[...].astype(BF16), wal_ref[...], preferred_element_type=F32) + bal_ref[...]
    g = (jnp.minimum(z, 0.0) - jnp.log(1.0 + jnp.exp(-jnp.abs(z)))) * (1.0 / GLA_TAU)
    row = lax.broadcasted_iota(jnp.int32, (CHUNK, CHUNK), 0)
    col = lax.broadcasted_iota(jnp.int32, (CHUNK, CHUNK), 1)
    keep = (col >= row) if rev else (col <= row)
    tri = keep.astype(F32)
    chunks = list(reversed(range(nch)) if rev else range(nch))
    heads = [(slice(h * GLA_DK, (h + 1) * GLA_DK), slice(h * GLA_DV, (h + 1) * GLA_DV)) for h in range(GLA_HEADS)]
    pre = {}
    for ci in chunks:
        rs = slice(ci * CHUNK, (ci + 1) * CHUNK)
        gc = g[rs]
        b = jnp.dot(tri, gc, precision=HIGHEST, preferred_element_type=F32)
        b_last = jnp.sum(gc, axis=0, keepdims=True)
        eb_last_t = jnp.transpose(jnp.broadcast_to(jnp.exp(b_last), (8, GLA_QK)))
        k = k_ref[rs, :].astype(F32)
        pre[ci] = dict(rs=rs, eb_last_t=eb_last_t,
                       qe=(q_ref[rs, :].astype(F32) * (GLA_DK ** -0.5) * jnp.exp(b)).astype(BF16),
                       ke=(k * jnp.exp(-b)).astype(BF16),
                       kd=(k * jnp.exp(b_last - b)).astype(BF16),
                       v=v_ref[rs, :].astype(BF16))
    for ci in chunks:
        c = pre[ci]
        c["a"] = [jnp.where(keep, lax.dot_general(c["qe"][:, ks], c["ke"][:, ks], (((1,), (1,)), ((), ())),
                                                  preferred_element_type=F32), 0.0).astype(BF16)
                  for ks, _ in heads]
    for ci in chunks:
        c = pre[ci]
        c["o"] = [jnp.dot(c["a"][h], c["v"][:, vs], preferred_element_type=F32) for h, (_, vs) in enumerate(heads)]
        c["upd"] = [lax.dot_general(c["kd"][:, ks], c["v"][:, vs], (((0,), (0,)), ((), ())),
                                    preferred_element_type=F32) for ks, vs in heads]
    state = [s_scr[h] for h in range(GLA_HEADS)]
    for ci in chunks:
        c = pre[ci]
        for h, (ks, vs) in enumerate(heads):
            o_ref[c["rs"], vs] = c["o"][h] + jnp.dot(c["qe"][:, ks], state[h].astype(BF16),
                                                     preferred_element_type=F32)
            state[h] = c["eb_last_t"][ks, 0:1] * state[h] + c["upd"][h]
    for h in range(GLA_HEADS):
        s_scr[h] = state[h]

    @pl.when(step == pl.num_programs(1) - 1)
    def _():
        sout_ref[0] = s_scr[...]


def _gla(y, a, w_alpha_pad, b_alpha, s0, batch, seq, rev):
    n = y.shape[0]
    nch = min(seq // CHUNK, 8)
    rows = nch * CHUNK
    ns = seq // rows

    def rowblk(b, s):
        return b * ns + (ns - 1 - s if rev else s)

    state_spec = pl.BlockSpec((1, GLA_HEADS, GLA_DK, GLA_DV), lambda b, s: (b, 0, 0, 0))
    return pl.pallas_call(
        functools.partial(_gla_kernel, rev=rev, nch=nch),
        out_shape=(jax.ShapeDtypeStruct((n, GLA_VW), F32),
                   jax.ShapeDtypeStruct((batch, GLA_HEADS, GLA_DK, GLA_DV), F32)),
        grid=(batch, ns),
        in_specs=[pl.BlockSpec((rows, LANE), lambda b, s: (rowblk(b, s), 0)),
                  pl.BlockSpec((rows, GLA_QK), lambda b, s: (rowblk(b, s), COL_QA // GLA_QK)),
                  pl.BlockSpec((rows, GLA_QK), lambda b, s: (rowblk(b, s), COL_KA // GLA_QK)),
                  pl.BlockSpec((rows, GLA_VW), lambda b, s: (rowblk(b, s), COL_VA // GLA_VW)),
                  pl.BlockSpec((LANE, GLA_QK), lambda b, s: (0, 0)),
                  pl.BlockSpec((1, GLA_QK), lambda b, s: (0, 0)),
                  state_spec],
        out_specs=(pl.BlockSpec((rows, GLA_VW), lambda b, s: (rowblk(b, s), 0)), state_spec),
        scratch_shapes=[pltpu.VMEM((GLA_HEADS, GLA_DK, GLA_DV), F32)],
        compiler_params=_cparams(("parallel", "arbitrary")),
        name="gla_bwd" if rev else "gla_fwd",
    )(a, y, y, y, w_alpha_pad, b_alpha, s0)


def _rope(x, cos_t, sin_a, sin_b):
    return x * cos_t + pltpu.roll(x, LANE - ROPE_P, 1) * sin_a + pltpu.roll(x, ROPE_P, 1) * sin_b


def _lambda(lam_ref, lam_init):
    lq1, lk1, lq2, lk2 = lam_ref[0:1], lam_ref[1:2], lam_ref[2:3], lam_ref[3:4]
    return (jnp.exp(jnp.sum(lq1 * lk1, axis=-1, keepdims=True))
            - jnp.exp(jnp.sum(lq2 * lk2, axis=-1, keepdims=True)) + lam_init)


ATTN_SUB = 128


def _attn_core(q, k, v, lam, g_ref, o_ref, lam_init):
    tq = q.shape[0]
    sub = min(tq, ATTN_SUB)
    lane = lax.broadcasted_iota(jnp.int32, (sub, LANE), 1)
    blocks = [slice(i * sub, (i + 1) * sub) for i in range(tq // sub)]
    s2s = []
    for rows in blocks:
        qs = q[rows] * (DIFF_DH ** -0.5 * math.log2(math.e))
        q2 = jnp.concatenate([jnp.where(lane < DIFF_DH, qs, 0.0), jnp.where(lane >= DIFF_DH, qs, 0.0)],
                             axis=0).astype(BF16)
        s2s.append(lax.dot_general(q2, k, (((1,), (1,)), ((), ())), preferred_element_type=F32))
    ps = [jnp.exp2(s2 - jnp.max(s2, axis=-1, keepdims=True)) for s2 in s2s]
    rs = [1.0 / jnp.sum(p, axis=-1, keepdims=True) for p in ps]
    pvs = [jnp.dot(p.astype(BF16), v, preferred_element_type=F32) for p in ps]
    for rows, pv, r in zip(blocks, pvs, rs):
        pv = pv * r
        o = pv[:sub] - lam * pv[sub:]
        ms = jnp.mean(o * o, axis=-1, keepdims=True)
        o_ref[rows, :] = (o * lax.rsqrt(ms + EPS) * g_ref[...] * (1.0 - lam_init)).astype(o_ref.dtype)


def _attn_ctx_kernel(q_ref, k_ref, v_ref, lam_ref, g_ref, o_ref, *, lam_init):
    lam = _lambda(lam_ref, lam_init)
    _attn_core(q_ref[...].astype(F32), k_ref[...].astype(BF16), v_ref[...].astype(BF16), lam, g_ref, o_ref, lam_init)


def _attn_lat_kernel(q_ref, k_ref, v_ref, kc_ref, vc_ref, cq_ref, saq_ref, sbq_ref, ck_ref, sak_ref, sbk_ref,
                     lam_ref, g_ref, o_ref, k_scr, v_scr, *, seq, lam_init):
    @pl.when(pl.program_id(2) == 0)
    def _():
        k_scr[0:seq] = _rope(k_ref[...].astype(F32), ck_ref[...], sak_ref[...], sbk_ref[...]).astype(BF16)
        k_scr[seq:] = kc_ref[...].astype(BF16)
        v_scr[0:seq] = v_ref[...].astype(BF16)
        v_scr[seq:] = vc_ref[...].astype(BF16)

    lam = _lambda(lam_ref, lam_init)
    q = _rope(q_ref[...].astype(F32), cq_ref[...], saq_ref[...], sbq_ref[...])
    _attn_core(q, k_scr[...], v_scr[...], lam, g_ref, o_ref, lam_init)


def _diff_attn_ctx(y, lam_vecs, subln_g, batch, seq, lam_init):
    n = y.shape[0]
    tq = seq
    hb = lambda col: col // LANE
    return pl.pallas_call(
        functools.partial(_attn_ctx_kernel, lam_init=lam_init),
        out_shape=jax.ShapeDtypeStruct((n, DIFF_HEADS * LANE), BF16),
        grid=(batch, DIFF_HEADS),
        in_specs=[pl.BlockSpec((tq, LANE), lambda b, h: (b, hb(COL_QD) + h)),
                  pl.BlockSpec((seq, LANE), lambda b, h: (b, hb(COL_KD) + h)),
                  pl.BlockSpec((seq, LANE), lambda b, h: (b, hb(COL_VD) + h)),
                  pl.BlockSpec((4, DIFF_DH), lambda b, h: (0, 0)),
                  pl.BlockSpec((1, LANE), lambda b, h: (0, 0))],
        out_specs=pl.BlockSpec((tq, LANE), lambda b, h: (b, h)),
        compiler_params=_cparams(("parallel", "parallel")),
        name="diff_attn_ctx",
    )(y, y, y, lam_vecs, subln_g)


def _diff_attn_lat(y, k_cache, v_cache, tabs, lam_vecs, subln_g, batch, seq, past, tq, lam_init):
    n = y.shape[0]
    nq = seq // tq
    hb = lambda col: col // LANE
    cos_t, sin_a, sin_b = tabs
    qtab = pl.BlockSpec((tq, LANE), lambda b, h, i: (i, 0))
    ktab = pl.BlockSpec((seq, LANE), lambda b, h, i: (0, 0))
    return pl.pallas_call(
        functools.partial(_attn_lat_kernel, seq=seq, lam_init=lam_init),
        out_shape=jax.ShapeDtypeStruct((n, DIFF_HEADS * LANE), BF16),
        grid=(batch, DIFF_HEADS, nq),
        in_specs=[pl.BlockSpec((tq, LANE), lambda b, h, i: (b * nq + i, hb(COL_QD) + h)),
                  pl.BlockSpec((seq, LANE), lambda b, h, i: (b, hb(COL_KD) + h)),
                  pl.BlockSpec((seq, LANE), lambda b, h, i: (b, hb(COL_VD) + h)),
                  pl.BlockSpec((past, LANE), lambda b, h, i: (b, h)),
                  pl.BlockSpec((past, LANE), lambda b, h, i: (b, h)),
                  qtab, qtab, qtab, ktab, ktab, ktab,
                  pl.BlockSpec((4, DIFF_DH), lambda b, h, i: (0, 0)),
                  pl.BlockSpec((1, LANE), lambda b, h, i: (0, 0))],
        out_specs=pl.BlockSpec((tq, LANE), lambda b, h, i: (b * nq + i, h)),
        scratch_shapes=[pltpu.VMEM((seq + past, LANE), BF16), pltpu.VMEM((seq + past, LANE), BF16)],
        compiler_params=_cparams(("parallel", "parallel", "arbitrary")),
        name="diff_attn_lat",
    )(y, y, y, k_cache, v_cache, cos_t, sin_a, sin_b, cos_t, sin_a, sin_b, lam_vecs, subln_g)


def _rope_tables(seq):
    rows = seq // GRID_W
    pos_r = jnp.repeat(jnp.arange(rows, dtype=F32), GRID_W)
    pos_c = jnp.tile(jnp.arange(GRID_W, dtype=F32), rows)
    inv = ROPE_BASE ** (-jnp.arange(ROPE_P, dtype=F32) / ROPE_P)
    ang_r = pos_r[:, None] * inv
    ang_c = pos_c[:, None] * inv
    zeros = jnp.zeros((seq, ROPE_P), F32)

    def axis_tabs(ang):
        cs, sn = jnp.cos(ang), jnp.sin(ang)
        return (jnp.concatenate([cs, cs], -1), jnp.concatenate([-sn, zeros], -1), jnp.concatenate([zeros, sn], -1))

    tr, tc = axis_tabs(ang_r), axis_tabs(ang_c)
    reps = LANE // (2 * ROPE_AXIS_DIM)
    return tuple(jnp.tile(jnp.concatenate([a, b], -1), (1, reps)) for a, b in zip(tr, tc))


MERGE_SUB = 128


def _merge_kernel(of_ref, ob_ref, ra_ref, od_ref, ga_ref, gb_ref, x_ref, mod_ref, gn_ref, wpa_ref, wpb_ref,
                  wout_ref, l1g_ref, l1b_ref, wr_ref, x1_ref, h2_ref, aff_ref, *, alpha):
    mod = mod_ref[0]
    tm = x_ref.shape[0]
    sub = min(tm, MERGE_SUB)
    blocks = [slice(i * sub, (i + 1) * sub) for i in range(tm // sub)]

    def gla_out(rows):
        o = of_ref[rows, :] + ob_ref[rows, :]
        parts = []
        for h in range(GLA_HEADS):
            oh = o[:, h * GLA_DV:(h + 1) * GLA_DV]
            parts.append(oh * lax.rsqrt(jnp.mean(oh * oh, axis=-1, keepdims=True) + EPS) * gn_ref[...])
        return (jnp.concatenate(parts, axis=-1) * _silu(ra_ref[rows, :].astype(F32))).astype(BF16)

    o_as = [gla_out(rows) for rows in blocks]
    pas = [jnp.dot(o_a, wpa_ref[...], preferred_element_type=F32) for o_a in o_as]
    pbs = [jnp.dot(od_ref[rows, :], wpb_ref[...], preferred_element_type=F32) for rows in blocks]
    mergeds = [(_sigmoid(ga_ref[rows, :].astype(F32)) * pa + _sigmoid(gb_ref[rows, :].astype(F32)) * pb).astype(BF16)
               for rows, pa, pb in zip(blocks, pas, pbs)]
    mixes = [jnp.dot(merged, wout_ref[...], preferred_element_type=F32) for merged in mergeds]
    for rows, mix in zip(blocks, mixes):
        x1 = _ln_plain(alpha * x_ref[rows, :] + mod[2:3] * mix) * l1g_ref[...] + l1b_ref[...]
        x1_ref[rows, :] = x1
        h2 = _ln_plain(x1) * (1.0 + mod[4:5]) + mod[3:4]
        h2_ref[rows, :] = h2.astype(h2_ref.dtype)
        logits = lax.dot_general(wr_ref[...], h2, (((1,), (1,)), ((), ())), precision=HIGHEST,
                                 preferred_element_type=F32)
        e = jnp.exp(logits - jnp.max(logits, axis=0, keepdims=True))
        aff_ref[:, rows] = e / jnp.sum(e, axis=0, keepdims=True)


def _merge(o_f, o_b, y, o_d, x, mod_all, mod_row, gn, wpa, wpb, wout, l1g, l1b, wr_t, tm, alpha):
    n, d = x.shape
    cb = lambda col: col // d
    rowspec = lambda col: pl.BlockSpec((tm, d), lambda i: (i, col))
    const = lambda shape: pl.BlockSpec(shape, lambda i: (0,) * len(shape))
    return pl.pallas_call(
        functools.partial(_merge_kernel, alpha=alpha),
        out_shape=(jax.ShapeDtypeStruct((n, d), F32), jax.ShapeDtypeStruct((n, d), F32),
                   jax.ShapeDtypeStruct((N_EXPERTS, n), F32)),
        grid=(n // tm,),
        in_specs=[rowspec(0), rowspec(0),
                  rowspec(cb(COL_RA)), rowspec(0), rowspec(cb(COL_GA)), rowspec(cb(COL_GB)), rowspec(0),
                  pl.BlockSpec((1, 6, d), lambda i: (mod_row(i, tm), 0, 0)),
                  const((1, GLA_DV)), const((d, d)), const((d, d)), const((d, d)),
                  const((1, d)), const((1, d)), const((N_EXPERTS, d))],
        out_specs=(pl.BlockSpec((tm, d), lambda i: (i, 0)), pl.BlockSpec((tm, d), lambda i: (i, 0)),
                   pl.BlockSpec((N_EXPERTS, tm), lambda i: (0, i))),
        compiler_params=_cparams(("parallel",)),
        name="merge",
    )(o_f, o_b, y, o_d, y, y, x, mod_all, gn, wpa, wpb, wout, l1g, l1b, wr_t)


TM_ID, TM_ONE, TM_GHI, TM_GMID, TM_GLO, TM_POS = 0, 16, 32, 48, 64, 80
EXP_BISECT = 7
LIN_BISECT = 50


def _route_kernel(aff_ref, bind_ref, tm_ref, pos_ref, offb_ref, *, cap):
    n_exp, n = aff_ref.shape
    nb = n // LANE
    aff = aff_ref[...]
    capf = float(cap)

    def count_ge(thr):
        return jnp.sum(jnp.where(aff >= thr, 1.0, 0.0), axis=1, keepdims=True)

    def exp_step(_, c):
        klo, khi = c
        kmid = jnp.floor((klo + khi) * 0.5)
        ok = count_ge(jnp.exp2(kmid)) >= capf
        return jnp.where(ok, kmid, klo), jnp.where(ok, khi, kmid)

    klo, khi = lax.fori_loop(0, EXP_BISECT, exp_step,
                             (jnp.full((n_exp, 1), -127.0, F32), jnp.full((n_exp, 1), 1.0, F32)))
    lo0 = jnp.where(klo <= -127.0, 0.0, jnp.exp2(klo))
    hi0 = jnp.exp2(khi)

    def lin_step(_, c):
        lo, hi = c
        mid = lo + (hi - lo) * 0.5
        ok = count_ge(mid) >= capf
        return jnp.where(ok, mid, lo), jnp.where(ok, hi, mid)

    lo, hi = lax.fori_loop(0, LIN_BISECT, lin_step, (lo0, hi0))

    row = lax.broadcasted_iota(jnp.int32, (LANE, LANE), 0)
    col = lax.broadcasted_iota(jnp.int32, (LANE, LANE), 1)
    incl = jnp.where(row <= col, 1.0, 0.0).astype(BF16)
    excl = jnp.where(row < col, 1.0, 0.0).astype(BF16)
    bind = bind_ref[...]

    def prefix(mask):
        xb = jnp.where(mask, 1.0, 0.0).astype(BF16)
        stacked = jnp.concatenate([xb[:, j * LANE:(j + 1) * LANE] for j in range(nb)], axis=0)
        within = jnp.dot(stacked, incl, preferred_element_type=F32)
        within = jnp.concatenate([within[j * n_exp:(j + 1) * n_exp, :] for j in range(nb)], axis=1)
        block_tot = jnp.dot(xb, bind, preferred_element_type=F32)
        block_off = jnp.dot(block_tot.astype(BF16), excl, preferred_element_type=F32)
        off_hi = jnp.floor(block_off * (1.0 / LANE))
        off_lo = block_off - off_hi * LANE
        nt_dims = (((1,), (1,)), ((), ()))
        spread = (lax.dot_general(off_hi.astype(BF16), bind, nt_dims, preferred_element_type=F32) * LANE
                  + lax.dot_general(off_lo.astype(BF16), bind, nt_dims, preferred_element_type=F32))
        return within + spread, block_off

    above = aff >= hi
    tied = jnp.logical_and(aff >= lo, jnp.logical_not(above))
    need = capf - jnp.sum(jnp.where(above, 1.0, 0.0), axis=1, keepdims=True)
    tie_rank, _ = prefix(tied)
    sel = jnp.logical_or(above, jnp.logical_and(tied, tie_rank <= need))
    cum, block_off = prefix(sel)
    pos = jnp.where(sel, cum - 1.0, -1.0)
    pos_ref[...] = pos
    offb_ref[...] = block_off

    g_hi = aff.astype(BF16).astype(F32)
    rem = aff - g_hi
    g_mid = rem.astype(BF16).astype(F32)
    g_lo = (rem - g_mid).astype(BF16).astype(F32)
    lane_id = lax.broadcasted_iota(jnp.int32, (n_exp, LANE), 1).astype(F32)
    ones = jnp.ones((n_exp, LANE), F32)
    fill = jnp.zeros((LANE - 6 * n_exp, LANE), F32)
    for j in range(nb):
        cs = slice(j * LANE, (j + 1) * LANE)
        local = lane_id + float(LANE * (j % (COMBINE_TM // LANE)))
        fields = jnp.concatenate([local, ones, g_hi[:, cs], g_mid[:, cs], g_lo[:, cs], pos[:, cs], fill], axis=0)
        tm_ref[cs, :] = jnp.transpose(fields)


def _route(aff_t, cap):
    n_exp, n = aff_t.shape
    bind = (jnp.arange(n, dtype=jnp.int32)[:, None] // LANE == jnp.arange(LANE, dtype=jnp.int32)[None, :]).astype(BF16)
    return pl.pallas_call(
        functools.partial(_route_kernel, cap=cap),
        out_shape=(jax.ShapeDtypeStruct((n, LANE), F32), jax.ShapeDtypeStruct((n_exp, n), F32),
                   jax.ShapeDtypeStruct((n_exp, LANE), F32)),
        compiler_params=pltpu.CompilerParams(vmem_limit_bytes=VMEM_LIMIT),
        name="route",
    )(aff_t, bind)


def _compact_kernel(base_ref, npass_ref, pos_ref, tm_ref, acc_ref, *, cap):
    t = pl.program_id(0)
    nt = pl.num_programs(0)
    n_exp = pos_ref.shape[0]

    @pl.when(t == 0)
    def _():
        acc_ref[...] = jnp.zeros(acc_ref.shape, F32)

    pos = pos_ref[...]
    table = tm_ref[...].astype(BF16)
    lane = lax.broadcasted_iota(jnp.int32, (1, LANE), 1)
    tile_scale = jnp.where(jnp.logical_and(lane >= TM_ONE, lane < TM_GHI), t.astype(F32), 1.0)
    slot_iota = lax.broadcasted_iota(jnp.int32, (WIN, pos.shape[1]), 0).astype(F32)

    def one_pass(p, carry):
        onehot = jnp.concatenate(
            [jnp.where(pos[e:e + 1, :] - (base_ref[e * nt + t] + p * WIN).astype(F32) == slot_iota, 1.0, 0.0)
             for e in range(n_exp)], axis=0).astype(BF16)
        rec = jnp.dot(onehot, table, preferred_element_type=F32) * tile_scale
        for e in range(n_exp):
            start = pl.multiple_of(jnp.minimum(base_ref[e * nt + t] + p * WIN, cap), 8)
            mine = jnp.logical_and(lane % n_exp == e, lane < TM_POS)
            acc_ref[pl.ds(start, WIN), :] += jnp.where(mine, rec[e * WIN:(e + 1) * WIN, :], 0.0)
        return carry

    lax.fori_loop(0, npass_ref[t], one_pass, 0)


def _compact(pos, tm, base, npass, cap):
    n_exp, n = pos.shape
    tm_rows = COMBINE_TM
    grid_spec = pltpu.PrefetchScalarGridSpec(
        num_scalar_prefetch=2,
        grid=(n // tm_rows,),
        in_specs=[pl.BlockSpec((n_exp, tm_rows), lambda i, *_: (0, i)),
                  pl.BlockSpec((tm_rows, LANE), lambda i, *_: (i, 0))],
        out_specs=pl.BlockSpec((cap + WIN, LANE), lambda i, *_: (0, 0)))
    return pl.pallas_call(
        functools.partial(_compact_kernel, cap=cap),
        out_shape=jax.ShapeDtypeStruct((cap + WIN, LANE), F32),
        grid_spec=grid_spec,
        compiler_params=_cparams(("arbitrary",)),
        name="compact",
    )(base.reshape(-1), npass, pos, tm)


def _routing(aff_t, cap):
    n_exp, n = aff_t.shape
    tm, pos, offb = _route(aff_t, cap)
    blocks_per_tile = COMBINE_TM // LANE
    off = offb[:, :n // LANE].astype(jnp.int32)[:, ::blocks_per_tile]
    nxt = jnp.concatenate([off[:, 1:], jnp.full((n_exp, 1), cap, jnp.int32)], axis=1)
    base = (off // 8) * 8
    npass = jnp.maximum(jnp.max(-(-(nxt - base) // WIN), axis=0), 1).astype(jnp.int32)
    acc = _compact(pos, tm, base, npass, cap)[:cap]
    idx = (acc[:, TM_ID:TM_ID + n_exp] + COMBINE_TM * acc[:, TM_ONE:TM_ONE + n_exp]).T.astype(jnp.int32)
    gate = (acc[:, TM_GHI:TM_GHI + n_exp] + acc[:, TM_GMID:TM_GMID + n_exp] + acc[:, TM_GLO:TM_GLO + n_exp]).T
    return idx, gate, tm, base, npass


def _moe_kernel(idx_ref, hp_hbm, hs_hbm, gate_ref, wg_ref, wu_ref, wd_ref, yp_ref, ys_ref, xf_scr, xb_scr, sem,
                *, rows, cap_p):
    e = pl.program_id(0)
    f = pl.program_id(1)
    nf = pl.num_programs(1)
    cap = xf_scr.shape[0]

    @pl.when(f == 0)
    def _():
        def issue_from(h_hbm):
            def issue(r, carry):
                tok = idx_ref[e * cap + r]
                pltpu.make_async_copy(h_hbm.at[pl.ds(tok, 1)], xf_scr.at[pl.ds(r, 1)], sem).start()
                return carry
            return issue

        lax.fori_loop(0, cap_p, issue_from(hp_hbm), 0, unroll=8)
        lax.fori_loop(cap_p, cap, issue_from(hs_hbm), 0, unroll=8)

        def drain(r, carry):
            pltpu.make_async_copy(hp_hbm.at[pl.ds(0, 1)], xf_scr.at[pl.ds(r, 1)], sem).wait()
            return carry

        lax.fori_loop(0, cap, drain, 0, unroll=8)
        xb_scr[...] = xf_scr[...].astype(BF16)

    wg = wg_ref[0].astype(BF16)
    wu = wu_ref[0].astype(BF16)
    wd = wd_ref[0].astype(BF16)
    for t in range(cap // rows):
        lo = t * rows
        out_ref, olo = (yp_ref, lo) if lo < cap_p else (ys_ref, lo - cap_p)
        xr = xb_scr[lo:lo + rows, :]
        hg = jnp.dot(xr, wg, preferred_element_type=F32)
        hu = jnp.dot(xr, wu, preferred_element_type=F32)
        hid = (_silu(hg) * hu).astype(BF16)
        prev = jnp.where(f > 0, out_ref[0, olo:olo + rows, :], 0.0)
        out_ref[0, olo:olo + rows, :] = prev + jnp.dot(hid, wd, preferred_element_type=F32)

    @pl.when(f == nf - 1)
    def _():
        cap_s = cap - cap_p
        yp_ref[0, 0:cap_p, :] = yp_ref[0, 0:cap_p, :] * gate_ref[0, 0:cap_p, :]
        ys_ref[0, 0:cap_s, :] = ys_ref[0, 0:cap_s, :] * gate_ref[0, cap_p:, :]
        yp_ref[0, cap_p:, :] = jnp.zeros((WIN, yp_ref.shape[2]), F32)
        ys_ref[0, cap_s:, :] = jnp.zeros((WIN, ys_ref.shape[2]), F32)


def _moe(idx, hp, hs, gate, wg, wu, wd, cap_p, ft, rows):
    e, cap = idx.shape
    cap_s = cap - cap_p
    d = hp.shape[1]
    ff = wg.shape[2]
    grid_spec = pltpu.PrefetchScalarGridSpec(
        num_scalar_prefetch=1,
        grid=(e, ff // ft),
        in_specs=[pl.BlockSpec(memory_space=pl.ANY),
                  pl.BlockSpec(memory_space=pl.ANY),
                  pl.BlockSpec((1, cap, 1), lambda i, f, idx_ref: (i, 0, 0)),
                  pl.BlockSpec((1, d, ft), lambda i, f, idx_ref: (i, 0, f)),
                  pl.BlockSpec((1, d, ft), lambda i, f, idx_ref: (i, 0, f)),
                  pl.BlockSpec((1, ft, d), lambda i, f, idx_ref: (i, f, 0))],
        out_specs=(pl.BlockSpec((1, cap_p + WIN, d), lambda i, f, idx_ref: (i, 0, 0)),
                   pl.BlockSpec((1, cap_s + WIN, d), lambda i, f, idx_ref: (i, 0, 0))),
        scratch_shapes=[pltpu.VMEM((cap, d), F32), pltpu.VMEM((cap, d), BF16), pltpu.SemaphoreType.DMA])
    return pl.pallas_call(
        functools.partial(_moe_kernel, rows=rows, cap_p=cap_p),
        out_shape=(jax.ShapeDtypeStruct((e, cap_p + WIN, d), F32), jax.ShapeDtypeStruct((e, cap_s + WIN, d), F32)),
        grid_spec=grid_spec,
        compiler_params=_cparams(("arbitrary", "arbitrary")),
        name="moe",
    )(idx.reshape(-1), hp, hs, gate, wg, wu, wd)


def _combine_kernel(base_ref, npass_ref, x1_ref, tm_ref, mod_ref, g_ref, b_ref, ye_hbm, o_ref,
                    ybuf, xbuf, sem, xsem, *, alpha):
    t = pl.program_id(0)
    nt = pl.num_programs(0)
    n_exp = ye_hbm.shape[0]
    tm = x1_ref.shape[0]
    cap = ye_hbm.shape[1] - WIN

    def window(tile, e, p, dst, s):
        start = pl.multiple_of(jnp.minimum(base_ref[e * nt + tile] + p * WIN, cap), 8)
        return pltpu.make_async_copy(ye_hbm.at[e, pl.ds(start, WIN), :], dst.at[pl.ds(e * WIN, WIN), :], s)

    def start_first_pass(tile, slot):
        for e in range(n_exp):
            window(tile, e, 0, ybuf.at[slot], sem.at[slot]).start()

    def expand(yw, p):
        lane = lax.broadcasted_iota(jnp.int32, (tm, LANE), 1).astype(F32)
        groups = []
        for gi in range(n_exp * WIN // LANE):
            ea, eb = 2 * gi, 2 * gi + 1
            ra = tm_ref[:, TM_POS + ea:TM_POS + ea + 1] - (base_ref[ea * nt + t] + p * WIN).astype(F32)
            rb = tm_ref[:, TM_POS + eb:TM_POS + eb + 1] - (base_ref[eb * nt + t] + (p - 1) * WIN).astype(F32)
            groups.append(jnp.where(lane == jnp.where(lane < WIN, ra, rb), 1.0, 0.0).astype(BF16))
        onehot = jnp.concatenate(groups, axis=1)
        hi = yw.astype(BF16)
        lo = (yw - hi.astype(F32)).astype(BF16)
        return (jnp.dot(onehot, hi, preferred_element_type=F32) + jnp.dot(onehot, lo, preferred_element_type=F32))

    @pl.when(t == 0)
    def _():
        start_first_pass(0, 0)

    slot = t % 2
    for e in range(n_exp):
        window(t, e, 0, ybuf.at[slot], sem.at[slot]).wait()

    @pl.when(t + 1 < nt)
    def _():
        start_first_pass(t + 1, 1 - slot)

    npass = npass_ref[t]

    def start_extra_pass(p):
        for e in range(n_exp):
            window(t, e, p, xbuf.at[p % 2], xsem.at[p % 2]).start()

    @pl.when(npass > 1)
    def _():
        start_extra_pass(1)

    acc = expand(ybuf[slot], 0)

    def extra_pass(p, acc):
        for e in range(n_exp):
            window(t, e, p, xbuf.at[p % 2], xsem.at[p % 2]).wait()

        @pl.when(p + 1 < npass)
        def _():
            start_extra_pass(p + 1)

        return acc + expand(xbuf[p % 2], p)

    acc = lax.fori_loop(1, npass, extra_pass, acc)
    mod = mod_ref[0]
    o_ref[...] = _ln_plain(alpha * x1_ref[...] + mod[5:6] * acc) * g_ref[...] + b_ref[...]


def _combine(x1, table, base, npass, ye, mod_all, mod_row, g, b, alpha):
    n, d = x1.shape
    n_exp = ye.shape[0]
    tm = COMBINE_TM
    grid_spec = pltpu.PrefetchScalarGridSpec(
        num_scalar_prefetch=2,
        grid=(n // tm,),
        in_specs=[pl.BlockSpec((tm, d), lambda i, *_: (i, 0)),
                  pl.BlockSpec((tm, LANE), lambda i, *_: (i, 0)),
                  pl.BlockSpec((1, 6, d), lambda i, *_: (mod_row(i, tm), 0, 0)),
                  pl.BlockSpec((1, d), lambda i, *_: (0, 0)), pl.BlockSpec((1, d), lambda i, *_: (0, 0)),
                  pl.BlockSpec(memory_space=pl.ANY)],
        out_specs=pl.BlockSpec((tm, d), lambda i, *_: (i, 0)),
        scratch_shapes=[pltpu.VMEM((2, n_exp * WIN, d), F32), pltpu.VMEM((2, n_exp * WIN, d), F32),
                        pltpu.SemaphoreType.DMA((2,)), pltpu.SemaphoreType.DMA((2,))])
    return pl.pallas_call(
        functools.partial(_combine_kernel, alpha=alpha),
        out_shape=jax.ShapeDtypeStruct((n, d), F32),
        grid_spec=grid_spec,
        compiler_params=_cparams(("arbitrary",)),
        name="combine",
    )(base.reshape(-1), npass, x1, table, mod_all, g, b, ye)


def _pick(n, pref):
    t = pref
    while n % t:
        t //= 2
    return t


def _mix_group(x3, mod_all, mod_row, lw, lam_init, alpha, s0, ctx):
    batch, seq, d = x3.shape
    n = batch * seq
    x = x3.reshape(n, d)
    tile_span = n if ctx is None else seq
    y_dtype = F32 if ctx is None else BF16
    y, a = _in_proj(x, mod_all, mod_row, lw["w_main"], lw["w_a"], _pick(tile_span, 1024), 1024, y_dtype)
    o_f, s_f = _gla(y, a, lw["w_alpha_pad"][0], lw["b_alpha"][0], s0[0], batch, seq, False)
    o_b, s_b = _gla(y, a, lw["w_alpha_pad"][1], lw["b_alpha"][1], s0[1], batch, seq, True)
    if ctx is None:
        o_d = _diff_attn_ctx(y, lw["lam_vecs"], lw["subln_g"], batch, seq, lam_init)
    else:
        k_cache, v_cache = ctx
        past = k_cache.shape[0] // batch
        o_d = _diff_attn_lat(y, k_cache, v_cache, _rope_tables(seq), lw["lam_vecs"], lw["subln_g"],
                             batch, seq, past, _pick(seq, 1024), lam_init)
    tm = _pick(seq, 256)
    x1, h2, aff_t = _merge(o_f, o_b, y, o_d, x, mod_all, mod_row, lw["gla_norm_g"], lw["w_proj_a"], lw["w_proj_b"],
                           lw["w_out"], lw["ln1_g"], lw["ln1_b"], lw["w_router_t"], tm, alpha)
    idx, gate, table, base, npass = _routing(aff_t, CAPACITY_FACTOR * n // N_EXPERTS)
    return dict(x1=x1, h2=h2, gate=gate, idx=idx, table=table, base=base, npass=npass, y=y, states=(s_f, s_b),
                mod_row=mod_row, shape=(batch, seq, d))


def _ffn_groups(gp, gs, mod_all, lw, alpha):
    cap_p, cap_s = gp["idx"].shape[1], gs["idx"].shape[1]
    idx = jnp.concatenate([gp["idx"], gs["idx"]], axis=1)
    gate = jnp.concatenate([gp["gate"], gs["gate"]], axis=1)[..., None]
    ye_p, ye_s = _moe(idx, gp["h2"], gs["h2"], gate, lw["w_exp_gate"], lw["w_exp_up"], lw["w_exp_down"],
                      cap_p, 256, _pick(math.gcd(cap_p, cap_s), 512))
    outs = []
    for g, ye in ((gp, ye_p), (gs, ye_s)):
        batch, seq, d = g["shape"]
        x2 = _combine(g["x1"], g["table"], g["base"], g["npass"], ye, mod_all, g["mod_row"],
                      lw["ln2_g"], lw["ln2_b"], alpha)
        outs.append(x2.reshape(batch, seq, d))
    return outs


def kernel(x_prompt, x_sample, cache_diff_k, cache_diff_v, state_gla_fwd, state_gla_bwd, c, c_ctx, w_mod, b_mod, w_in, w_alpha_f, b_alpha_f, w_alpha_b, b_alpha_b, gla_norm_g, w_proj_a, lambda_q1, lambda_k1, lambda_q2, lambda_k2, subln_g, w_proj_b, w_out, ln1_g, ln1_b, w_router, w_exp_gate, w_exp_up, w_exp_down, ln2_g, ln2_b):
    depth = w_in.shape[0]
    bp, sp, d = x_prompt.shape
    bs, ss, _ = x_sample.shape
    alpha = (2.0 * depth) ** 0.25
    xp, xs = x_prompt, x_sample

    n_mod = 1 + bs
    mod_rows = -(-n_mod // 8) * 8
    c_all = jnp.zeros((mod_rows, d), F32).at[0].set(c_ctx).at[1:n_mod].set(c)

    new_k, new_v, new_f, new_b = [], [], [], []
    for l in range(depth):
        lam_init = 0.8 - 0.6 * math.exp(-0.3 * l)
        mod_all = _modulation(c_all, w_mod[l], b_mod[l]).reshape(mod_rows, 6, d)
        wi = w_in[l]
        o = 0
        pieces = {}
        for name, width in zip(("qa", "ka", "va", "ra", "af", "ab", "qd", "kd", "vd", "ga", "gb"),
                               (GLA_QK, GLA_QK, GLA_VW, GLA_VW, GLA_RANK, GLA_RANK, d, d, d, d, d)):
            pieces[name] = wi[:, o:o + width]
            o += width
        w_main = jnp.concatenate([pieces[k] for k in ("qa", "ka", "va", "ra", "qd", "kd", "vd", "ga", "gb")],
                                 axis=1).astype(BF16)
        w_a = jnp.concatenate([pieces["af"], pieces["ab"], jnp.zeros((d, LANE - 2 * GLA_RANK), F32)],
                              axis=1).astype(BF16)
        w_alpha_pad = jnp.zeros((2, LANE, GLA_QK), F32)
        w_alpha_pad = w_alpha_pad.at[0, 0:GLA_RANK].set(w_alpha_f[l]).at[1, GLA_RANK:2 * GLA_RANK].set(w_alpha_b[l])
        lw = dict(
            w_main=w_main, w_a=w_a, w_alpha_pad=w_alpha_pad.astype(BF16),
            b_alpha=jnp.stack([b_alpha_f[l], b_alpha_b[l]])[:, None, :],
            lam_vecs=jnp.stack([lambda_q1[l], lambda_k1[l], lambda_q2[l], lambda_k2[l]]),
            subln_g=subln_g[l][None, :], gla_norm_g=gla_norm_g[l][None, :],
            w_proj_a=w_proj_a[l].astype(BF16), w_proj_b=w_proj_b[l].astype(BF16), w_out=w_out[l].astype(BF16),
            ln1_g=ln1_g[l][None, :], ln1_b=ln1_b[l][None, :], w_router_t=w_router[l].T,
            w_exp_gate=w_exp_gate[l], w_exp_up=w_exp_up[l], w_exp_down=w_exp_down[l],
            ln2_g=ln2_g[l][None, :], ln2_b=ln2_b[l][None, :])

        zeros = jnp.zeros((2, bp, GLA_HEADS, GLA_DK, GLA_DV), F32)
        gp = _mix_group(xp, mod_all, lambda i, tm: 0, lw, lam_init, alpha, zeros, None)
        y_p = gp["y"]
        new_k.append(y_p[:, COL_KD:COL_KD + d].reshape(bp, sp, DIFF_HEADS, 2, DIFF_DH))
        new_v.append(y_p[:, COL_VD:COL_VD + d].reshape(bp, sp, DIFF_HEADS, 2 * DIFF_DH))
        new_f.append(gp["states"][0])
        new_b.append(gp["states"][1])

        s0 = (state_gla_fwd[:, l], state_gla_bwd[:, l])
        past = cache_diff_k.shape[2]
        ctx = (cache_diff_k[:, l].reshape(bs * past, d), cache_diff_v[:, l].reshape(bs * past, d))
        gs = _mix_group(xs, mod_all, lambda i, tm: 1 + (i * tm) // ss, lw, lam_init, alpha, s0, ctx)
        xp, xs = _ffn_groups(gp, gs, mod_all, lw, alpha)

    return (xp, xs, jnp.stack(new_k, axis=1), jnp.stack(new_v, axis=1),
            jnp.stack(new_f, axis=1), jnp.stack(new_b, axis=1))
```

```python
import functools
import math

import jax
import jax.numpy as jnp
from jax import lax
from jax.experimental import pallas as pl
from jax.experimental.pallas import tpu as pltpu

F32 = jnp.float32
BF16 = jnp.bfloat16
HIGHEST = lax.Precision.HIGHEST

LANE = 128
EPS = 1e-6
GRID_W = 64
GLA_HEADS = 4
GLA_DK = 128
GLA_DV = 256
GLA_QK = GLA_HEADS * GLA_DK
GLA_VW = GLA_HEADS * GLA_DV
GLA_RANK = 16
GLA_TAU = 16.0
CHUNK = 64
DIFF_HEADS = 8
DIFF_DH = 64
ROPE_BASE = 10000.0
ROPE_AXIS_DIM = DIFF_DH // 2
ROPE_P = ROPE_AXIS_DIM // 2
N_EXPERTS = 16
CAPACITY_FACTOR = 2
VMEM_LIMIT = 56 * 1024 * 1024
COMBINE_TM = 256
WIN = 64
GATHER_G = 64

COL_QA, COL_KA, COL_VA, COL_RA, COL_QD, COL_KD, COL_VD, COL_GA, COL_GB = (
    0, 512, 1024, 2048, 3072, 4096, 5120, 6144, 7168)
MAIN_W = 8192


def _cparams(sem):
    return pltpu.CompilerParams(dimension_semantics=sem, vmem_limit_bytes=VMEM_LIMIT)


def _ln_plain(x):
    mu = jnp.mean(x, axis=-1, keepdims=True)
    xc = x - mu
    var = jnp.mean(xc * xc, axis=-1, keepdims=True)
    return xc * lax.rsqrt(var + EPS)


def _sigmoid(x):
    return 1.0 / (1.0 + jnp.exp(-x))


def _silu(x):
    return x * _sigmoid(x)


def _mod_kernel(c_ref, w_ref, b_ref, o_ref):
    c = c_ref[...]
    o_ref[...] = jnp.dot(_silu(c), w_ref[...], precision=HIGHEST, preferred_element_type=F32) + b_ref[...]


def _modulation(c_all, w_mod, b_mod):
    rows, d = c_all.shape
    n = w_mod.shape[1]
    tn = 1024
    return pl.pallas_call(
        _mod_kernel,
        out_shape=jax.ShapeDtypeStruct((rows, n), F32),
        grid=(n // tn,),
        in_specs=[pl.BlockSpec((rows, d), lambda j: (0, 0)),
                  pl.BlockSpec((d, tn), lambda j: (0, j)),
                  pl.BlockSpec((1, tn), lambda j: (0, j))],
        out_specs=pl.BlockSpec((rows, tn), lambda j: (0, j)),
        compiler_params=_cparams(("parallel",)),
        name="modulation",
    )(c_all, w_mod, b_mod.reshape(1, n))


def _in_proj_kernel(x_ref, mod_ref, w_ref, wa_ref, y_ref, a_ref, h_scr):
    @pl.when(pl.program_id(1) == 0)
    def _():
        mod = mod_ref[0]
        h = _ln_plain(x_ref[...]) * (1.0 + mod[1:2]) + mod[0:1]
        hb = h.astype(BF16)
        h_scr[...] = hb
        a_ref[...] = jnp.dot(hb, wa_ref[...], preferred_element_type=F32)

    y_ref[...] = jnp.dot(h_scr[...], w_ref[...], preferred_element_type=F32).astype(y_ref.dtype)


def _in_proj(x, mod_all, mod_row, w_main, w_a, tm, tn, y_dtype):
    n, d = x.shape
    nw = w_main.shape[1]
    return pl.pallas_call(
        _in_proj_kernel,
        out_shape=(jax.ShapeDtypeStruct((n, nw), y_dtype), jax.ShapeDtypeStruct((n, LANE), F32)),
        grid=(n // tm, nw // tn),
        in_specs=[pl.BlockSpec((tm, d), lambda i, j: (i, 0)),
                  pl.BlockSpec((1, 6, d), lambda i, j: (mod_row(i, tm), 0, 0)),
                  pl.BlockSpec((d, tn), lambda i, j: (0, j)),
                  pl.BlockSpec((d, LANE), lambda i, j: (0, 0))],
        out_specs=(pl.BlockSpec((tm, tn), lambda i, j: (i, j)),
                   pl.BlockSpec((tm, LANE), lambda i, j: (i, 0))),
        scratch_shapes=[pltpu.VMEM((tm, d), BF16)],
        compiler_params=_cparams(("parallel", "arbitrary")),
        name="in_proj",
    )(x, mod_all, w_main, w_a)


def _gla_kernel(a_ref, q_ref, k_ref, v_ref, wal_ref, bal_ref, s0_ref, o_ref, sout_ref, s_scr, *, rev, nch):
    step = pl.program_id(1)

    @pl.when(step == 0)
    def _():
        s_scr[...] = s0_ref[0]

    z = jnp.dot(a_ref[...].astype(BF16), wal_ref[...], preferred_element_type=F32) + bal_ref[...]
    g = (jnp.minimum(z, 0.0) - jnp.log(1.0 + jnp.exp(-jnp.abs(z)))) * (1.0 / GLA_TAU)
    row = lax.broadcasted_iota(jnp.int32, (CHUNK, CHUNK), 0)
    col = lax.broadcasted_iota(jnp.int32, (CHUNK, CHUNK), 1)
    keep = (col >= row) if rev else (col <= row)
    tri = keep.astype(F32)
    chunks = list(reversed(range(nch)) if rev else range(nch))
    heads = [(slice(h * GLA_DK, (h + 1) * GLA_DK), slice(h * GLA_DV, (h + 1) * GLA_DV)) for h in range(GLA_HEADS)]
    pre = {}
    for ci in chunks:
        rs = slice(ci * CHUNK, (ci + 1) * CHUNK)
        gc = g[rs]
        b = jnp.dot(tri, gc, precision=HIGHEST, preferred_element_type=F32)
        b_last = jnp.sum(gc, axis=0, keepdims=True)
        eb_last_t = jnp.transpose(jnp.broadcast_to(jnp.exp(b_last), (8, GLA_QK)))
        k = k_ref[rs, :].astype(F32)
        pre[ci] = dict(rs=rs, eb_last_t=eb_last_t,
                       qe=(q_ref[rs, :].astype(F32) * (GLA_DK ** -0.5) * jnp.exp(b)).astype(BF16),
                       ke=(k * jnp.exp(-b)).astype(BF16),
                       kd=(k * jnp.exp(b_last - b)).astype(BF16),
                       v=v_ref[rs, :].astype(BF16))
    for ci in chunks:
        c = pre[ci]
        c["a"] = [jnp.where(keep, lax.dot_general(c["qe"][:, ks], c["ke"][:, ks], (((1,), (1,)), ((), ())),
                                                  preferred_element_type=F32), 0.0).astype(BF16)
                  for ks, _ in heads]
    for ci in chunks:
        c = pre[ci]
        c["o"] = [jnp.dot(c["a"][h], c["v"][:, vs], preferred_element_type=F32) for h, (_, vs) in enumerate(heads)]
        c["upd"] = [lax.dot_general(c["kd"][:, ks], c["v"][:, vs], (((0,), (0,)), ((), ())),
                                    preferred_element_type=F32) for ks, vs in heads]
    state = [s_scr[h] for h in range(GLA_HEADS)]
    for ci in chunks:
        c = pre[ci]
        for h, (ks, vs) in enumerate(heads):
            o_ref[c["rs"], vs] = c["o"][h] + jnp.dot(c["qe"][:, ks], state[h].astype(BF16),
                                                     preferred_element_type=F32)
            state[h] = c["eb_last_t"][ks, 0:1] * state[h] + c["upd"][h]
    for h in range(GLA_HEADS):
        s_scr[h] = state[h]

    @pl.when(step == pl.num_programs(1) - 1)
    def _():
        sout_ref[0] = s_scr[...]


def _gla(y, a, w_alpha_pad, b_alpha, s0, batch, seq, rev):
    n = y.shape[0]
    nch = min(seq // CHUNK, 8)
    rows = nch * CHUNK
    ns = seq // rows

    def rowblk(b, s):
        return b * ns + (ns - 1 - s if rev else s)

    state_spec = pl.BlockSpec((1, GLA_HEADS, GLA_DK, GLA_DV), lambda b, s: (b, 0, 0, 0))
    return pl.pallas_call(
        functools.partial(_gla_kernel, rev=rev, nch=nch),
        out_shape=(jax.ShapeDtypeStruct((n, GLA_VW), F32),
                   jax.ShapeDtypeStruct((batch, GLA_HEADS, GLA_DK, GLA_DV), F32)),
        grid=(batch, ns),
        in_specs=[pl.BlockSpec((rows, LANE), lambda b, s: (rowblk(b, s), 0)),
                  pl.BlockSpec((rows, GLA_QK), lambda b, s: (rowblk(b, s), COL_QA // GLA_QK)),
                  pl.BlockSpec((rows, GLA_QK), lambda b, s: (rowblk(b, s), COL_KA // GLA_QK)),
                  pl.BlockSpec((rows, GLA_VW), lambda b, s: (rowblk(b, s), COL_VA // GLA_VW)),
                  pl.BlockSpec((LANE, GLA_QK), lambda b, s: (0, 0)),
                  pl.BlockSpec((1, GLA_QK), lambda b, s: (0, 0)),
                  state_spec],
        out_specs=(pl.BlockSpec((rows, GLA_VW), lambda b, s: (rowblk(b, s), 0)), state_spec),
        scratch_shapes=[pltpu.VMEM((GLA_HEADS, GLA_DK, GLA_DV), F32)],
        compiler_params=_cparams(("parallel", "arbitrary")),
        name="gla_bwd" if rev else "gla_fwd",
    )(a, y, y, y, w_alpha_pad, b_alpha, s0)


def _rope(x, cos_t, sin_a, sin_b):
    return x * cos_t + pltpu.roll(x, LANE - ROPE_P, 1) * sin_a + pltpu.roll(x, ROPE_P, 1) * sin_b


def _lambda(lam_ref, lam_init):
    lq1, lk1, lq2, lk2 = lam_ref[0:1], lam_ref[1:2], lam_ref[2:3], lam_ref[3:4]
    return (jnp.exp(jnp.sum(lq1 * lk1, axis=-1, keepdims=True))
            - jnp.exp(jnp.sum(lq2 * lk2, axis=-1, keepdims=True)) + lam_init)


ATTN_SUB = 128


def _attn_core(q, k, v, lam, g_ref, o_ref, lam_init):
    tq = q.shape[0]
    sub = min(tq, ATTN_SUB)
    lane = lax.broadcasted_iota(jnp.int32, (sub, LANE), 1)
    blocks = [slice(i * sub, (i + 1) * sub) for i in range(tq // sub)]
    s2s = []
    for rows in blocks:
        qs = q[rows] * (DIFF_DH ** -0.5 * math.log2(math.e))
        q2 = jnp.concatenate([jnp.where(lane < DIFF_DH, qs, 0.0), jnp.where(lane >= DIFF_DH, qs, 0.0)],
                             axis=0).astype(BF16)
        s2s.append(lax.dot_general(q2, k, (((1,), (1,)), ((), ())), preferred_element_type=F32))
    ps = [jnp.exp2(s2 - jnp.max(s2, axis=-1, keepdims=True)) for s2 in s2s]
    rs = [1.0 / jnp.sum(p, axis=-1, keepdims=True) for p in ps]
    pvs = [jnp.dot(p.astype(BF16), v, preferred_element_type=F32) for p in ps]
    for rows, pv, r in zip(blocks, pvs, rs):
        pv = pv * r
        o = pv[:sub] - lam * pv[sub:]
        ms = jnp.mean(o * o, axis=-1, keepdims=True)
        o_ref[rows, :] = (o * lax.rsqrt(ms + EPS) * g_ref[...] * (1.0 - lam_init)).astype(o_ref.dtype)


def _attn_ctx_kernel(q_ref, k_ref, v_ref, lam_ref, g_ref, o_ref, *, lam_init):
    lam = _lambda(lam_ref, lam_init)
    _attn_core(q_ref[...].astype(F32), k_ref[...].astype(BF16), v_ref[...].astype(BF16), lam, g_ref, o_ref, lam_init)


def _attn_lat_kernel(q_ref, k_ref, v_ref, kc_ref, vc_ref, cq_ref, saq_ref, sbq_ref, ck_ref, sak_ref, sbk_ref,
                     lam_ref, g_ref, o_ref, k_scr, v_scr, *, seq, lam_init):
    @pl.when(pl.program_id(2) == 0)
    def _():
        k_scr[0:seq] = _rope(k_ref[...].astype(F32), ck_ref[...], sak_ref[...], sbk_ref[...]).astype(BF16)
        k_scr[seq:] = kc_ref[...].astype(BF16)
        v_scr[0:seq] = v_ref[...].astype(BF16)
        v_scr[seq:] = vc_ref[...].astype(BF16)

    lam = _lambda(lam_ref, lam_init)
    q = _rope(q_ref[...].astype(F32), cq_ref[...], saq_ref[...], sbq_ref[...])
    _attn_core(q, k_scr[...], v_scr[...], lam, g_ref, o_ref, lam_init)


def _diff_attn_ctx(y, lam_vecs, subln_g, batch, seq, lam_init):
    n = y.shape[0]
    tq = seq
    hb = lambda col: col // LANE
    return pl.pallas_call(
        functools.partial(_attn_ctx_kernel, lam_init=lam_init),
        out_shape=jax.ShapeDtypeStruct((n, DIFF_HEADS * LANE), BF16),
        grid=(batch, DIFF_HEADS),
        in_specs=[pl.BlockSpec((tq, LANE), lambda b, h: (b, hb(COL_QD) + h)),
                  pl.BlockSpec((seq, LANE), lambda b, h: (b, hb(COL_KD) + h)),
                  pl.BlockSpec((seq, LANE), lambda b, h: (b, hb(COL_VD) + h)),
                  pl.BlockSpec((4, DIFF_DH), lambda b, h: (0, 0)),
                  pl.BlockSpec((1, LANE), lambda b, h: (0, 0))],
        out_specs=pl.BlockSpec((tq, LANE), lambda b, h: (b, h)),
        compiler_params=_cparams(("parallel", "parallel")),
        name="diff_attn_ctx",
    )(y, y, y, lam_vecs, subln_g)


def _diff_attn_lat(y, k_cache, v_cache, tabs, lam_vecs, subln_g, batch, seq, past, tq, lam_init):
    n = y.shape[0]
    nq = seq // tq
    hb = lambda col: col // LANE
    cos_t, sin_a, sin_b = tabs
    qtab = pl.BlockSpec((tq, LANE), lambda b, h, i: (i, 0))
    ktab = pl.BlockSpec((seq, LANE), lambda b, h, i: (0, 0))
    return pl.pallas_call(
        functools.partial(_attn_lat_kernel, seq=seq, lam_init=lam_init),
        out_shape=jax.ShapeDtypeStruct((n, DIFF_HEADS * LANE), BF16),
        grid=(batch, DIFF_HEADS, nq),
        in_specs=[pl.BlockSpec((tq, LANE), lambda b, h, i: (b * nq + i, hb(COL_QD) + h)),
                  pl.BlockSpec((seq, LANE), lambda b, h, i: (b, hb(COL_KD) + h)),
                  pl.BlockSpec((seq, LANE), lambda b, h, i: (b, hb(COL_VD) + h)),
                  pl.BlockSpec((past, LANE), lambda b, h, i: (b, h)),
                  pl.BlockSpec((past, LANE), lambda b, h, i: (b, h)),
                  qtab, qtab, qtab, ktab, ktab, ktab,
                  pl.BlockSpec((4, DIFF_DH), lambda b, h, i: (0, 0)),
                  pl.BlockSpec((1, LANE), lambda b, h, i: (0, 0))],
        out_specs=pl.BlockSpec((tq, LANE), lambda b, h, i: (b * nq + i, h)),
        scratch_shapes=[pltpu.VMEM((seq + past, LANE), BF16), pltpu.VMEM((seq + past, LANE), BF16)],
        compiler_params=_cparams(("parallel", "parallel", "arbitrary")),
        name="diff_attn_lat",
    )(y, y, y, k_cache, v_cache, cos_t, sin_a, sin_b, cos_t, sin_a, sin_b, lam_vecs, subln_g)


def _rope_tables(seq):
    rows = seq // GRID_W
    pos_r = jnp.repeat(jnp.arange(rows, dtype=F32), GRID_W)
    pos_c = jnp.tile(jnp.arange(GRID_W, dtype=F32), rows)
    inv = ROPE_BASE ** (-jnp.arange(ROPE_P, dtype=F32) / ROPE_P)
    ang_r = pos_r[:, None] * inv
    ang_c = pos_c[:, None] * inv
    zeros = jnp.zeros((seq, ROPE_P), F32)

    def axis_tabs(ang):
        cs, sn = jnp.cos(ang), jnp.sin(ang)
        return (jnp.concatenate([cs, cs], -1), jnp.concatenate([-sn, zeros], -1), jnp.concatenate([zeros, sn], -1))

    tr, tc = axis_tabs(ang_r), axis_tabs(ang_c)
    reps = LANE // (2 * ROPE_AXIS_DIM)
    return tuple(jnp.tile(jnp.concatenate([a, b], -1), (1, reps)) for a, b in zip(tr, tc))


MERGE_SUB = 128


def _merge_kernel(of_ref, ob_ref, ra_ref, od_ref, ga_ref, gb_ref, x_ref, mod_ref, gn_ref, wpa_ref, wpb_ref,
                  wout_ref, l1g_ref, l1b_ref, wr_ref, x1_ref, h2_ref, aff_ref, *, alpha):
    mod = mod_ref[0]
    tm = x_ref.shape[0]
    sub = min(tm, MERGE_SUB)
    blocks = [slice(i * sub, (i + 1) * sub) for i in range(tm // sub)]

    def gla_out(rows):
        o = of_ref[rows, :] + ob_ref[rows, :]
        parts = []
        for h in range(GLA_HEADS):
            oh = o[:, h * GLA_DV:(h + 1) * GLA_DV]
            parts.append(oh * lax.rsqrt(jnp.mean(oh * oh, axis=-1, keepdims=True) + EPS) * gn_ref[...])
        return (jnp.concatenate(parts, axis=-1) * _silu(ra_ref[rows, :].astype(F32))).astype(BF16)

    o_as = [gla_out(rows) for rows in blocks]
    pas = [jnp.dot(o_a, wpa_ref[...], preferred_element_type=F32) for o_a in o_as]
    pbs = [jnp.dot(od_ref[rows, :], wpb_ref[...], preferred_element_type=F32) for rows in blocks]
    mergeds = [(_sigmoid(ga_ref[rows, :].astype(F32)) * pa + _sigmoid(gb_ref[rows, :].astype(F32)) * pb).astype(BF16)
               for rows, pa, pb in zip(blocks, pas, pbs)]
    mixes = [jnp.dot(merged, wout_ref[...], preferred_element_type=F32) for merged in mergeds]
    for rows, mix in zip(blocks, mixes):
        x1 = _ln_plain(alpha * x_ref[rows, :] + mod[2:3] * mix) * l1g_ref[...] + l1b_ref[...]
        x1_ref[rows, :] = x1
        h2 = _ln_plain(x1) * (1.0 + mod[4:5]) + mod[3:4]
        h2_ref[rows, :] = h2.astype(h2_ref.dtype)
        logits = lax.dot_general(wr_ref[...], h2, (((1,), (1,)), ((), ())), precision=HIGHEST,
                                 preferred_element_type=F32)
        e = jnp.exp(logits - jnp.max(logits, axis=0, keepdims=True))
        aff_ref[:, rows] = e / jnp.sum(e, axis=0, keepdims=True)


def _merge(o_f, o_b, y, o_d, x, mod_all, mod_row, gn, wpa, wpb, wout, l1g, l1b, wr_t, tm, alpha):
    n, d = x.shape
    cb = lambda col: col // d
    rowspec = lambda col: pl.BlockSpec((tm, d), lambda i: (i, col))
    const = lambda shape: pl.BlockSpec(shape, lambda i: (0,) * len(shape))
    return pl.pallas_call(
        functools.partial(_merge_kernel, alpha=alpha),
        out_shape=(jax.ShapeDtypeStruct((n, d), F32), jax.ShapeDtypeStruct((n, d), F32),
                   jax.ShapeDtypeStruct((N_EXPERTS, n), F32)),
        grid=(n // tm,),
        in_specs=[rowspec(0), rowspec(0),
                  rowspec(cb(COL_RA)), rowspec(0), rowspec(cb(COL_GA)), rowspec(cb(COL_GB)), rowspec(0),
                  pl.BlockSpec((1, 6, d), lambda i: (mod_row(i, tm), 0, 0)),
                  const((1, GLA_DV)), const((d, d)), const((d, d)), const((d, d)),
                  const((1, d)), const((1, d)), const((N_EXPERTS, d))],
        out_specs=(pl.BlockSpec((tm, d), lambda i: (i, 0)), pl.BlockSpec((tm, d), lambda i: (i, 0)),
                   pl.BlockSpec((N_EXPERTS, tm), lambda i: (0, i))),
        compiler_params=_cparams(("parallel",)),
        name="merge",
    )(o_f, o_b, y, o_d, y, y, x, mod_all, gn, wpa, wpb, wout, l1g, l1b, wr_t)


TM_ID, TM_ONE, TM_GHI, TM_GMID, TM_GLO, TM_POS = 0, 16, 32, 48, 64, 80
EXP_BISECT = 7
LIN_BISECT = 50


def _route_kernel(aff_ref, bind_ref, tm_ref, pos_ref, offb_ref, *, cap):
    n_exp, n = aff_ref.shape
    nb = n // LANE
    aff = aff_ref[...]
    capf = float(cap)

    def count_ge(thr):
        return jnp.sum(jnp.where(aff >= thr, 1.0, 0.0), axis=1, keepdims=True)

    def exp_step(_, c):
        klo, khi = c
        kmid = jnp.floor((klo + khi) * 0.5)
        ok = count_ge(jnp.exp2(kmid)) >= capf
        return jnp.where(ok, kmid, klo), jnp.where(ok, khi, kmid)

    klo, khi = lax.fori_loop(0, EXP_BISECT, exp_step,
                             (jnp.full((n_exp, 1), -127.0, F32), jnp.full((n_exp, 1), 1.0, F32)))
    lo0 = jnp.where(klo <= -127.0, 0.0, jnp.exp2(klo))
    hi0 = jnp.exp2(khi)

    def lin_step(_, c):
        lo, hi = c
        mid = lo + (hi - lo) * 0.5
        ok = count_ge(mid) >= capf
        return jnp.where(ok, mid, lo), jnp.where(ok, hi, mid)

    lo, hi = lax.fori_loop(0, LIN_BISECT, lin_step, (lo0, hi0))

    row = lax.broadcasted_iota(jnp.int32, (LANE, LANE), 0)
    col = lax.broadcasted_iota(jnp.int32, (LANE, LANE), 1)
    incl = jnp.where(row <= col, 1.0, 0.0).astype(BF16)
    excl = jnp.where(row < col, 1.0, 0.0).astype(BF16)
    bind = bind_ref[...]

    def prefix(mask):
        xb = jnp.where(mask, 1.0, 0.0).astype(BF16)
        stacked = jnp.concatenate([xb[:, j * LANE:(j + 1) * LANE] for j in range(nb)], axis=0)
        within = jnp.dot(stacked, incl, preferred_element_type=F32)
        within = jnp.concatenate([within[j * n_exp:(j + 1) * n_exp, :] for j in range(nb)], axis=1)
        block_tot = jnp.dot(xb, bind, preferred_element_type=F32)
        block_off = jnp.dot(block_tot.astype(BF16), excl, preferred_element_type=F32)
        off_hi = jnp.floor(block_off * (1.0 / LANE))
        off_lo = block_off - off_hi * LANE
        nt_dims = (((1,), (1,)), ((), ()))
        spread = (lax.dot_general(off_hi.astype(BF16), bind, nt_dims, preferred_element_type=F32) * LANE
                  + lax.dot_general(off_lo.astype(BF16), bind, nt_dims, preferred_element_type=F32))
        return within + spread, block_off

    above = aff >= hi
    tied = jnp.logical_and(aff >= lo, jnp.logical_not(above))
    need = capf - jnp.sum(jnp.where(above, 1.0, 0.0), axis=1, keepdims=True)
    tie_rank, _ = prefix(tied)
    sel = jnp.logical_or(above, jnp.logical_and(tied, tie_rank <= need))
    cum, block_off = prefix(sel)
    pos = jnp.where(sel, cum - 1.0, -1.0)
    pos_ref[...] = pos
    offb_ref[...] = block_off

    g_hi = aff.astype(BF16).astype(F32)
    rem = aff - g_hi
    g_mid = rem.astype(BF16).astype(F32)
    g_lo = (rem - g_mid).astype(BF16).astype(F32)
    lane_id = lax.broadcasted_iota(jnp.int32, (n_exp, LANE), 1).astype(F32)
    ones = jnp.ones((n_exp, LANE), F32)
    fill = jnp.zeros((LANE - 6 * n_exp, LANE), F32)
    for j in range(nb):
        cs = slice(j * LANE, (j + 1) * LANE)
        local = lane_id + float(LANE * (j % (COMBINE_TM // LANE)))
        fields = jnp.concatenate([local, ones, g_hi[:, cs], g_mid[:, cs], g_lo[:, cs], pos[:, cs], fill], axis=0)
        tm_ref[cs, :] = jnp.transpose(fields)


def _route(aff_t, cap):
    n_exp, n = aff_t.shape
    bind = (jnp.arange(n, dtype=jnp.int32)[:, None] // LANE == jnp.arange(LANE, dtype=jnp.int32)[None, :]).astype(BF16)
    return pl.pallas_call(
        functools.partial(_route_kernel, cap=cap),
        out_shape=(jax.ShapeDtypeStruct((n, LANE), F32), jax.ShapeDtypeStruct((n_exp, n), F32),
                   jax.ShapeDtypeStruct((n_exp, LANE), F32)),
        compiler_params=pltpu.CompilerParams(vmem_limit_bytes=VMEM_LIMIT),
        name="route",
    )(aff_t, bind)


def _compact_kernel(base_ref, npass_ref, pos_ref, tm_ref, acc_ref, *, cap):
    t = pl.program_id(0)
    nt = pl.num_programs(0)
    n_exp = pos_ref.shape[0]

    @pl.when(t == 0)
    def _():
        acc_ref[...] = jnp.zeros(acc_ref.shape, F32)

    pos = pos_ref[...]
    table = tm_ref[...].astype(BF16)
    lane = lax.broadcasted_iota(jnp.int32, (1, LANE), 1)
    tile_scale = jnp.where(jnp.logical_and(lane >= TM_ONE, lane < TM_GHI), t.astype(F32), 1.0)
    slot_iota = lax.broadcasted_iota(jnp.int32, (WIN, pos.shape[1]), 0).astype(F32)

    def one_pass(p, carry):
        onehot = jnp.concatenate(
            [jnp.where(pos[e:e + 1, :] - (base_ref[e * nt + t] + p * WIN).astype(F32) == slot_iota, 1.0, 0.0)
             for e in range(n_exp)], axis=0).astype(BF16)
        rec = jnp.dot(onehot, table, preferred_element_type=F32) * tile_scale
        for e in range(n_exp):
            start = pl.multiple_of(jnp.minimum(base_ref[e * nt + t] + p * WIN, cap), 8)
            mine = jnp.logical_and(lane % n_exp == e, lane < TM_POS)
            acc_ref[pl.ds(start, WIN), :] += jnp.where(mine, rec[e * WIN:(e + 1) * WIN, :], 0.0)
        return carry

    lax.fori_loop(0, npass_ref[t], one_pass, 0)


def _compact(pos, tm, base, npass, cap):
    n_exp, n = pos.shape
    tm_rows = COMBINE_TM
    grid_spec = pltpu.PrefetchScalarGridSpec(
        num_scalar_prefetch=2,
        grid=(n // tm_rows,),
        in_specs=[pl.BlockSpec((n_exp, tm_rows), lambda i, *_: (0, i)),
                  pl.BlockSpec((tm_rows, LANE), lambda i, *_: (i, 0))],
        out_specs=pl.BlockSpec((cap + WIN, LANE), lambda i, *_: (0, 0)))
    return pl.pallas_call(
        functools.partial(_compact_kernel, cap=cap),
        out_shape=jax.ShapeDtypeStruct((cap + WIN, LANE), F32),
        grid_spec=grid_spec,
        compiler_params=_cparams(("arbitrary",)),
        name="compact",
    )(base.reshape(-1), npass, pos, tm)


def _routing(aff_t, cap):
    n_exp, n = aff_t.shape
    tm, pos, offb = _route(aff_t, cap)
    blocks_per_tile = COMBINE_TM // LANE
    off = offb[:, :n // LANE].astype(jnp.int32)[:, ::blocks_per_tile]
    nxt = jnp.concatenate([off[:, 1:], jnp.full((n_exp, 1), cap, jnp.int32)], axis=1)
    base = (off // 8) * 8
    npass = jnp.maximum(jnp.max(-(-(nxt - base) // WIN), axis=0), 1).astype(jnp.int32)
    acc = _compact(pos, tm, base, npass, cap)[:cap]
    idx = (acc[:, TM_ID:TM_ID + n_exp] + COMBINE_TM * acc[:, TM_ONE:TM_ONE + n_exp]).T.astype(jnp.int32)
    gate = (acc[:, TM_GHI:TM_GHI + n_exp] + acc[:, TM_GMID:TM_GMID + n_exp] + acc[:, TM_GLO:TM_GLO + n_exp]).T
    return idx, gate, tm, base, npass


def _moe_kernel(idx_ref, hp_hbm, hs_hbm, gate_ref, wg_ref, wu_ref, wd_ref, yp_ref, ys_ref, xf_scr, xb_scr, sem,
                *, rows, cap_p, nf, s_tiles):
    e = pl.program_id(0)
    f = pl.program_id(1)
    n_exp = pl.num_programs(0)
    cap = xb_scr.shape[0]
    cap_s = cap - cap_p
    total = xf_scr.shape[0]

    def wait_rows(count):
        def drain(r, carry):
            pltpu.make_async_copy(hp_hbm.at[pl.ds(0, 1)], xf_scr.at[pl.ds(r, 1)], sem).wait()
            return carry
        lax.fori_loop(0, count, drain, 0, unroll=8)

    @pl.when(f == 0)
    def _():
        @pl.when(e == 0)
        def _():
            def issue_from(h_hbm):
                def issue(r, carry):
                    pltpu.make_async_copy(h_hbm.at[pl.ds(idx_ref[r], 1)], xf_scr.at[pl.ds(r, 1)], sem).start()
                    return carry
                return issue

            lax.fori_loop(0, cap_p, issue_from(hp_hbm), 0, unroll=8)
            lax.fori_loop(cap_p, cap, issue_from(hs_hbm), 0, unroll=8)
            wait_rows(cap)

        @pl.when(e > 0)
        def _():
            wait_rows(total)

        xb_scr[...] = xf_scr[0:cap, :].astype(BF16)

    nxt = jnp.minimum(e + 1, n_exp - 1) * cap

    def prefetch_group(h_hbm, group, n_groups, first_slot, spare_slot):
        real = group < n_groups
        src = first_slot + jnp.where(real, group, 0) * GATHER_G
        dst = jnp.where(real, first_slot + group * GATHER_G, spare_slot + (group - n_groups) * GATHER_G)
        for k in range(GATHER_G):
            pltpu.make_async_copy(h_hbm.at[pl.ds(idx_ref[nxt + src + k], 1)], xf_scr.at[pl.ds(dst + k, 1)],
                                  sem).start()

    p_groups, s_groups = cap_p // GATHER_G, cap_s // GATHER_G
    wg = wg_ref[0].astype(BF16)
    wu = wu_ref[0].astype(BF16)
    wd = wd_ref[0].astype(BF16)
    for t in range(cap // rows):
        if t == 0:
            prefetch_group(hp_hbm, f, p_groups, 0, cap)
        elif t <= s_tiles:
            prefetch_group(hs_hbm, f * s_tiles + (t - 1), s_groups, cap_p, cap + (nf - p_groups) * GATHER_G)
        lo = t * rows
        out_ref, olo = (yp_ref, lo) if lo < cap_p else (ys_ref, lo - cap_p)
        xr = xb_scr[lo:lo + rows, :]
        hg = jnp.dot(xr, wg, preferred_element_type=F32)
        hu = jnp.dot(xr, wu, preferred_element_type=F32)
        hid = (_silu(hg) * hu).astype(BF16)
        prev = jnp.where(f > 0, out_ref[0, olo:olo + rows, :], 0.0)
        out_ref[0, olo:olo + rows, :] = prev + jnp.dot(hid, wd, preferred_element_type=F32)

    @pl.when(jnp.logical_and(e == n_exp - 1, f == nf - 1))
    def _():
        wait_rows(total)

    @pl.when(f == nf - 1)
    def _():
        yp_ref[0, 0:cap_p, :] = yp_ref[0, 0:cap_p, :] * gate_ref[0, 0:cap_p, :]
        ys_ref[0, 0:cap_s, :] = ys_ref[0, 0:cap_s, :] * gate_ref[0, cap_p:, :]
        yp_ref[0, cap_p:, :] = jnp.zeros((WIN, yp_ref.shape[2]), F32)
        ys_ref[0, cap_s:, :] = jnp.zeros((WIN, ys_ref.shape[2]), F32)


def _moe(idx, hp, hs, gate, wg, wu, wd, cap_p, ft, rows):
    e, cap = idx.shape
    cap_s = cap - cap_p
    d = hp.shape[1]
    ff = wg.shape[2]
    nf = ff // ft
    p_groups, s_groups = cap_p // GATHER_G, cap_s // GATHER_G
    s_tiles = -(-s_groups // nf)
    assert cap_p % GATHER_G == 0 and cap_s % GATHER_G == 0 and p_groups <= nf and s_tiles <= cap // rows - 1
    gather_rows = nf * (1 + s_tiles) * GATHER_G
    grid_spec = pltpu.PrefetchScalarGridSpec(
        num_scalar_prefetch=1,
        grid=(e, ff // ft),
        in_specs=[pl.BlockSpec(memory_space=pl.ANY),
                  pl.BlockSpec(memory_space=pl.ANY),
                  pl.BlockSpec((1, cap, 1), lambda i, f, idx_ref: (i, 0, 0)),
                  pl.BlockSpec((1, d, ft), lambda i, f, idx_ref: (i, 0, f)),
                  pl.BlockSpec((1, d, ft), lambda i, f, idx_ref: (i, 0, f)),
                  pl.BlockSpec((1, ft, d), lambda i, f, idx_ref: (i, f, 0))],
        out_specs=(pl.BlockSpec((1, cap_p + WIN, d), lambda i, f, idx_ref: (i, 0, 0)),
                   pl.BlockSpec((1, cap_s + WIN, d), lambda i, f, idx_ref: (i, 0, 0))),
        scratch_shapes=[pltpu.VMEM((gather_rows, d), F32), pltpu.VMEM((cap, d), BF16), pltpu.SemaphoreType.DMA])
    return pl.pallas_call(
        functools.partial(_moe_kernel, rows=rows, cap_p=cap_p, nf=nf, s_tiles=s_tiles),
        out_shape=(jax.ShapeDtypeStruct((e, cap_p + WIN, d), F32), jax.ShapeDtypeStruct((e, cap_s + WIN, d), F32)),
        grid_spec=grid_spec,
        compiler_params=_cparams(("arbitrary", "arbitrary")),
        name="moe",
    )(idx.reshape(-1), hp, hs, gate, wg, wu, wd)


def _combine_kernel(base_ref, npass_ref, x1_ref, tm_ref, mod_ref, g_ref, b_ref, ye_hbm, o_ref,
                    ybuf, xbuf, sem, xsem, *, alpha):
    t = pl.program_id(0)
    nt = pl.num_programs(0)
    n_exp = ye_hbm.shape[0]
    tm = x1_ref.shape[0]
    cap = ye_hbm.shape[1] - WIN

    def window(tile, e, p, dst, s):
        start = pl.multiple_of(jnp.minimum(base_ref[e * nt + tile] + p * WIN, cap), 8)
        return pltpu.make_async_copy(ye_hbm.at[e, pl.ds(start, WIN), :], dst.at[pl.ds(e * WIN, WIN), :], s)

    def start_first_pass(tile, slot):
        for e in range(n_exp):
            window(tile, e, 0, ybuf.at[slot], sem.at[slot]).start()

    def expand(yw, p):
        lane = lax.broadcasted_iota(jnp.int32, (tm, LANE), 1).astype(F32)
        groups = []
        for gi in range(n_exp * WIN // LANE):
            ea, eb = 2 * gi, 2 * gi + 1
            ra = tm_ref[:, TM_POS + ea:TM_POS + ea + 1] - (base_ref[ea * nt + t] + p * WIN).astype(F32)
            rb = tm_ref[:, TM_POS + eb:TM_POS + eb + 1] - (base_ref[eb * nt + t] + (p - 1) * WIN).astype(F32)
            groups.append(jnp.where(lane == jnp.where(lane < WIN, ra, rb), 1.0, 0.0).astype(BF16))
        onehot = jnp.concatenate(groups, axis=1)
        hi = yw.astype(BF16)
        lo = (yw - hi.astype(F32)).astype(BF16)
        return (jnp.dot(onehot, hi, preferred_element_type=F32) + jnp.dot(onehot, lo, preferred_element_type=F32))

    @pl.when(t == 0)
    def _():
        start_first_pass(0, 0)

    slot = t % 2
    for e in range(n_exp):
        window(t, e, 0, ybuf.at[slot], sem.at[slot]).wait()

    @pl.when(t + 1 < nt)
    def _():
        start_first_pass(t + 1, 1 - slot)

    npass = npass_ref[t]

    def start_extra_pass(p):
        for e in range(n_exp):
            window(t, e, p, xbuf.at[p % 2], xsem.at[p % 2]).start()

    @pl.when(npass > 1)
    def _():
        start_extra_pass(1)

    acc = expand(ybuf[slot], 0)

    def extra_pass(p, acc):
        for e in range(n_exp):
            window(t, e, p, xbuf.at[p % 2], xsem.at[p % 2]).wait()

        @pl.when(p + 1 < npass)
        def _():
            start_extra_pass(p + 1)

        return acc + expand(xbuf[p % 2], p)

    acc = lax.fori_loop(1, npass, extra_pass, acc)
    mod = mod_ref[0]
    o_ref[...] = _ln_plain(alpha * x1_ref[...] + mod[5:6] * acc) * g_ref[...] + b_ref[...]


def _combine(x1, table, base, npass, ye, mod_all, mod_row, g, b, alpha):
    n, d = x1.shape
    n_exp = ye.shape[0]
    tm = COMBINE_TM
    grid_spec = pltpu.PrefetchScalarGridSpec(
        num_scalar_prefetch=2,
        grid=(n // tm,),
        in_specs=[pl.BlockSpec((tm, d), lambda i, *_: (i, 0)),
                  pl.BlockSpec((tm, LANE), lambda i, *_: (i, 0)),
                  pl.BlockSpec((1, 6, d), lambda i, *_: (mod_row(i, tm), 0, 0)),
                  pl.BlockSpec((1, d), lambda i, *_: (0, 0)), pl.BlockSpec((1, d), lambda i, *_: (0, 0)),
                  pl.BlockSpec(memory_space=pl.ANY)],
        out_specs=pl.BlockSpec((tm, d), lambda i, *_: (i, 0)),
        scratch_shapes=[pltpu.VMEM((2, n_exp * WIN, d), F32), pltpu.VMEM((2, n_exp * WIN, d), F32),
                        pltpu.SemaphoreType.DMA((2,)), pltpu.SemaphoreType.DMA((2,))])
    return pl.pallas_call(
        functools.partial(_combine_kernel, alpha=alpha),
        out_shape=jax.ShapeDtypeStruct((n, d), F32),
        grid_spec=grid_spec,
        compiler_params=_cparams(("arbitrary",)),
        name="combine",
    )(base.reshape(-1), npass, x1, table, mod_all, g, b, ye)


def _pick(n, pref):
    t = pref
    while n % t:
        t //= 2
    return t


def _mix_group(x3, mod_all, mod_row, lw, lam_init, alpha, s0, ctx):
    batch, seq, d = x3.shape
    n = batch * seq
    x = x3.reshape(n, d)
    tile_span = n if ctx is None else seq
    y_dtype = F32 if ctx is None else BF16
    y, a = _in_proj(x, mod_all, mod_row, lw["w_main"], lw["w_a"], _pick(tile_span, 1024), 1024, y_dtype)
    o_f, s_f = _gla(y, a, lw["w_alpha_pad"][0], lw["b_alpha"][0], s0[0], batch, seq, False)
    o_b, s_b = _gla(y, a, lw["w_alpha_pad"][1], lw["b_alpha"][1], s0[1], batch, seq, True)
    if ctx is None:
        o_d = _diff_attn_ctx(y, lw["lam_vecs"], lw["subln_g"], batch, seq, lam_init)
    else:
        k_cache, v_cache = ctx
        past = k_cache.shape[0] // batch
        o_d = _diff_attn_lat(y, k_cache, v_cache, _rope_tables(seq), lw["lam_vecs"], lw["subln_g"],
                             batch, seq, past, _pick(seq, 1024), lam_init)
    tm = _pick(seq, 256)
    x1, h2, aff_t = _merge(o_f, o_b, y, o_d, x, mod_all, mod_row, lw["gla_norm_g"], lw["w_proj_a"], lw["w_proj_b"],
                           lw["w_out"], lw["ln1_g"], lw["ln1_b"], lw["w_router_t"], tm, alpha)
    idx, gate, table, base, npass = _routing(aff_t, CAPACITY_FACTOR * n // N_EXPERTS)
    return dict(x1=x1, h2=h2, gate=gate, idx=idx, table=table, base=base, npass=npass, y=y, states=(s_f, s_b),
                mod_row=mod_row, shape=(batch, seq, d))


def _ffn_groups(gp, gs, mod_all, lw, alpha):
    cap_p, cap_s = gp["idx"].shape[1], gs["idx"].shape[1]
    idx = jnp.concatenate([gp["idx"], gs["idx"]], axis=1)
    gate = jnp.concatenate([gp["gate"], gs["gate"]], axis=1)[..., None]
    ye_p, ye_s = _moe(idx, gp["h2"], gs["h2"], gate, lw["w_exp_gate"], lw["w_exp_up"], lw["w_exp_down"],
                      cap_p, 256, _pick(math.gcd(cap_p, cap_s), 512))
    outs = []
    for g, ye in ((gp, ye_p), (gs, ye_s)):
        batch, seq, d = g["shape"]
        x2 = _combine(g["x1"], g["table"], g["base"], g["npass"], ye, mod_all, g["mod_row"],
                      lw["ln2_g"], lw["ln2_b"], alpha)
        outs.append(x2.reshape(batch, seq, d))
    return outs


def kernel(x_prompt, x_sample, cache_diff_k, cache_diff_v, state_gla_fwd, state_gla_bwd, c, c_ctx, w_mod, b_mod, w_in, w_alpha_f, b_alpha_f, w_alpha_b, b_alpha_b, gla_norm_g, w_proj_a, lambda_q1, lambda_k1, lambda_q2, lambda_k2, subln_g, w_proj_b, w_out, ln1_g, ln1_b, w_router, w_exp_gate, w_exp_up, w_exp_down, ln2_g, ln2_b):
    depth = w_in.shape[0]
    bp, sp, d = x_prompt.shape
    bs, ss, _ = x_sample.shape
    alpha = (2.0 * depth) ** 0.25
    xp, xs = x_prompt, x_sample

    n_mod = 1 + bs
    mod_rows = -(-n_mod // 8) * 8
    c_all = jnp.zeros((mod_rows, d), F32).at[0].set(c_ctx).at[1:n_mod].set(c)

    new_k, new_v, new_f, new_b = [], [], [], []
    for l in range(depth):
        lam_init = 0.8 - 0.6 * math.exp(-0.3 * l)
        mod_all = _modulation(c_all, w_mod[l], b_mod[l]).reshape(mod_rows, 6, d)
        wi = w_in[l]
        o = 0
        pieces = {}
        for name, width in zip(("qa", "ka", "va", "ra", "af", "ab", "qd", "kd", "vd", "ga", "gb"),
                               (GLA_QK, GLA_QK, GLA_VW, GLA_VW, GLA_RANK, GLA_RANK, d, d, d, d, d)):
            pieces[name] = wi[:, o:o + width]
            o += width
        w_main = jnp.concatenate([pieces[k] for k in ("qa", "ka", "va", "ra", "qd", "kd", "vd", "ga", "gb")],
                                 axis=1).astype(BF16)
        w_a = jnp.concatenate([pieces["af"], pieces["ab"], jnp.zeros((d, LANE - 2 * GLA_RANK), F32)],
                              axis=1).astype(BF16)
        w_alpha_pad = jnp.zeros((2, LANE, GLA_QK), F32)
        w_alpha_pad = w_alpha_pad.at[0, 0:GLA_RANK].set(w_alpha_f[l]).at[1, GLA_RANK:2 * GLA_RANK].set(w_alpha_b[l])
        lw = dict(
            w_main=w_main, w_a=w_a, w_alpha_pad=w_alpha_pad.astype(BF16),
            b_alpha=jnp.stack([b_alpha_f[l], b_alpha_b[l]])[:, None, :],
            lam_vecs=jnp.stack([lambda_q1[l], lambda_k1[l], lambda_q2[l], lambda_k2[l]]),
            subln_g=subln_g[l][None, :], gla_norm_g=gla_norm_g[l][None, :],
            w_proj_a=w_proj_a[l].astype(BF16), w_proj_b=w_proj_b[l].astype(BF16), w_out=w_out[l].astype(BF16),
            ln1_g=ln1_g[l][None, :], ln1_b=ln1_b[l][None, :], w_router_t=w_router[l].T,
            w_exp_gate=w_exp_gate[l], w_exp_up=w_exp_up[l], w_exp_down=w_exp_down[l],
            ln2_g=ln2_g[l][None, :], ln2_b=ln2_b[l][None, :])

        zeros = jnp.zeros((2, bp, GLA_HEADS, GLA_DK, GLA_DV), F32)
        gp = _mix_group(xp, mod_all, lambda i, tm: 0, lw, lam_init, alpha, zeros, None)
        y_p = gp["y"]
        new_k.append(y_p[:, COL_KD:COL_KD + d].reshape(bp, sp, DIFF_HEADS, 2, DIFF_DH))
        new_v.append(y_p[:, COL_VD:COL_VD + d].reshape(bp, sp, DIFF_HEADS, 2 * DIFF_DH))
        new_f.append(gp["states"][0])
        new_b.append(gp["states"][1])

        s0 = (state_gla_fwd[:, l], state_gla_bwd[:, l])
        past = cache_diff_k.shape[2]
        ctx = (cache_diff_k[:, l].reshape(bs * past, d), cache_diff_v[:, l].reshape(bs * past, d))
        gs = _mix_group(xs, mod_all, lambda i, tm: 1 + (i * tm) // ss, lw, lam_init, alpha, s0, ctx)
        xp, xs = _ffn_groups(gp, gs, mod_all, lw, alpha)

    return (xp, xs, jnp.stack(new_k, axis=1), jnp.stack(new_v, axis=1),
            jnp.stack(new_f, axis=1), jnp.stack(new_b, axis=1))
```

```python
import functools
import math

import jax
import jax.numpy as jnp
from jax import lax
from jax.experimental import pallas as pl
from jax.experimental.pallas import tpu as pltpu

F32 = jnp.float32
BF16 = jnp.bfloat16
HIGHEST = lax.Precision.HIGHEST

LANE = 128
EPS = 1e-6
GRID_W = 64
GLA_HEADS = 4
GLA_DK = 128
GLA_DV = 256
GLA_QK = GLA_HEADS * GLA_DK
GLA_VW = GLA_HEADS * GLA_DV
GLA_RANK = 16
GLA_TAU = 16.0
CHUNK = 64
DIFF_HEADS = 8
DIFF_DH = 64
ROPE_BASE = 10000.0
ROPE_AXIS_DIM = DIFF_DH // 2
ROPE_P = ROPE_AXIS_DIM // 2
N_EXPERTS = 16
CAPACITY_FACTOR = 2
VMEM_LIMIT = 56 * 1024 * 1024
COMBINE_TM = 256
WIN = 64
GATHER_G = 64

COL_QA, COL_KA, COL_VA, COL_RA, COL_QD, COL_KD, COL_VD, COL_GA, COL_GB = (
    0, 512, 1024, 2048, 3072, 4096, 5120, 6144, 7168)
MAIN_W = 8192


def _cparams(sem):
    return pltpu.CompilerParams(dimension_semantics=sem, vmem_limit_bytes=VMEM_LIMIT)


def _ln_plain(x):
    mu = jnp.mean(x, axis=-1, keepdims=True)
    xc = x - mu
    var = jnp.mean(xc * xc, axis=-1, keepdims=True)
    return xc * lax.rsqrt(var + EPS)


def _sigmoid(x):
    return 1.0 / (1.0 + jnp.exp(-x))


def _silu(x):
    return x * _sigmoid(x)


def _mod_kernel(c_ref, w_ref, b_ref, o_ref):
    c = c_ref[...]
    o_ref[...] = jnp.dot(_silu(c), w_ref[...], precision=HIGHEST, preferred_element_type=F32) + b_ref[...]


def _modulation(c_all, w_mod, b_mod):
    rows, d = c_all.shape
    n = w_mod.shape[1]
    tn = 1024
    return pl.pallas_call(
        _mod_kernel,
        out_shape=jax.ShapeDtypeStruct((rows, n), F32),
        grid=(n // tn,),
        in_specs=[pl.BlockSpec((rows, d), lambda j: (0, 0)),
                  pl.BlockSpec((d, tn), lambda j: (0, j)),
                  pl.BlockSpec((1, tn), lambda j: (0, j))],
        out_specs=pl.BlockSpec((rows, tn), lambda j: (0, j)),
        compiler_params=_cparams(("parallel",)),
        name="modulation",
    )(c_all, w_mod, b_mod.reshape(1, n))


def _in_proj_kernel(x_ref, mod_ref, w_ref, wa_ref, y_ref, a_ref, h_scr):
    @pl.when(pl.program_id(1) == 0)
    def _():
        mod = mod_ref[0]
        h = _ln_plain(x_ref[...]) * (1.0 + mod[1:2]) + mod[0:1]
        hb = h.astype(BF16)
        h_scr[...] = hb
        a_ref[...] = jnp.dot(hb, wa_ref[...], preferred_element_type=F32)

    y_ref[...] = jnp.dot(h_scr[...], w_ref[...], preferred_element_type=F32).astype(y_ref.dtype)


def _in_proj(x, mod_all, mod_row, w_main, w_a, tm, tn, y_dtype):
    n, d = x.shape
    nw = w_main.shape[1]
    return pl.pallas_call(
        _in_proj_kernel,
        out_shape=(jax.ShapeDtypeStruct((n, nw), y_dtype), jax.ShapeDtypeStruct((n, LANE), F32)),
        grid=(n // tm, nw // tn),
        in_specs=[pl.BlockSpec((tm, d), lambda i, j: (i, 0)),
                  pl.BlockSpec((1, 6, d), lambda i, j: (mod_row(i, tm), 0, 0)),
                  pl.BlockSpec((d, tn), lambda i, j: (0, j)),
                  pl.BlockSpec((d, LANE), lambda i, j: (0, 0))],
        out_specs=(pl.BlockSpec((tm, tn), lambda i, j: (i, j)),
                   pl.BlockSpec((tm, LANE), lambda i, j: (i, 0))),
        scratch_shapes=[pltpu.VMEM((tm, d), BF16)],
        compiler_params=_cparams(("parallel", "arbitrary")),
        name="in_proj",
    )(x, mod_all, w_main, w_a)


def _gla_kernel(a_ref, q_ref, k_ref, v_ref, wal_ref, bal_ref, s0_ref, o_ref, sout_ref, s_scr, *, rev, nch):
    step = pl.program_id(1)

    @pl.when(step == 0)
    def _():
        s_scr[...] = s0_ref[0]

    z = jnp.dot(a_ref[...].astype(BF16), wal_ref[...], preferred_element_type=F32) + bal_ref[...]
    g = (jnp.minimum(z, 0.0) - jnp.log(1.0 + jnp.exp(-jnp.abs(z)))) * (1.0 / GLA_TAU)
    row = lax.broadcasted_iota(jnp.int32, (CHUNK, CHUNK), 0)
    col = lax.broadcasted_iota(jnp.int32, (CHUNK, CHUNK), 1)
    keep = (col >= row) if rev else (col <= row)
    tri = keep.astype(F32)
    chunks = list(reversed(range(nch)) if rev else range(nch))
    heads = [(slice(h * GLA_DK, (h + 1) * GLA_DK), slice(h * GLA_DV, (h + 1) * GLA_DV)) for h in range(GLA_HEADS)]
    pre = {}
    for ci in chunks:
        rs = slice(ci * CHUNK, (ci + 1) * CHUNK)
        gc = g[rs]
        b = jnp.dot(tri, gc, precision=HIGHEST, preferred_element_type=F32)
        b_last = jnp.sum(gc, axis=0, keepdims=True)
        eb_last_t = jnp.transpose(jnp.broadcast_to(jnp.exp(b_last), (8, GLA_QK)))
        k = k_ref[rs, :].astype(F32)
        pre[ci] = dict(rs=rs, eb_last_t=eb_last_t,
                       qe=(q_ref[rs, :].astype(F32) * (GLA_DK ** -0.5) * jnp.exp(b)).astype(BF16),
                       ke=(k * jnp.exp(-b)).astype(BF16),
                       kd=(k * jnp.exp(b_last - b)).astype(BF16),
                       v=v_ref[rs, :].astype(BF16))
    for ci in chunks:
        c = pre[ci]
        c["a"] = [jnp.where(keep, lax.dot_general(c["qe"][:, ks], c["ke"][:, ks], (((1,), (1,)), ((), ())),
                                                  preferred_element_type=F32), 0.0).astype(BF16)
                  for ks, _ in heads]
    for ci in chunks:
        c = pre[ci]
        c["o"] = [jnp.dot(c["a"][h], c["v"][:, vs], preferred_element_type=F32) for h, (_, vs) in enumerate(heads)]
        c["upd"] = [lax.dot_general(c["kd"][:, ks], c["v"][:, vs], (((0,), (0,)), ((), ())),
                                    preferred_element_type=F32) for ks, vs in heads]
    state = [s_scr[h] for h in range(GLA_HEADS)]
    for ci in chunks:
        c = pre[ci]
        for h, (ks, vs) in enumerate(heads):
            o_ref[c["rs"], vs] = c["o"][h] + jnp.dot(c["qe"][:, ks], state[h].astype(BF16),
                                                     preferred_element_type=F32)
            state[h] = c["eb_last_t"][ks, 0:1] * state[h] + c["upd"][h]
    for h in range(GLA_HEADS):
        s_scr[h] = state[h]

    @pl.when(step == pl.num_programs(1) - 1)
    def _():
        sout_ref[0] = s_scr[...]


def _gla(y, a, w_alpha_pad, b_alpha, s0, batch, seq, rev):
    n = y.shape[0]
    nch = min(seq // CHUNK, 16)
    rows = nch * CHUNK
    ns = seq // rows

    def rowblk(b, s):
        return b * ns + (ns - 1 - s if rev else s)

    state_spec = pl.BlockSpec((1, GLA_HEADS, GLA_DK, GLA_DV), lambda b, s: (b, 0, 0, 0))
    return pl.pallas_call(
        functools.partial(_gla_kernel, rev=rev, nch=nch),
        out_shape=(jax.ShapeDtypeStruct((n, GLA_VW), F32),
                   jax.ShapeDtypeStruct((batch, GLA_HEADS, GLA_DK, GLA_DV), F32)),
        grid=(batch, ns),
        in_specs=[pl.BlockSpec((rows, LANE), lambda b, s: (rowblk(b, s), 0)),
                  pl.BlockSpec((rows, GLA_QK), lambda b, s: (rowblk(b, s), COL_QA // GLA_QK)),
                  pl.BlockSpec((rows, GLA_QK), lambda b, s: (rowblk(b, s), COL_KA // GLA_QK)),
                  pl.BlockSpec((rows, GLA_VW), lambda b, s: (rowblk(b, s), COL_VA // GLA_VW)),
                  pl.BlockSpec((LANE, GLA_QK), lambda b, s: (0, 0)),
                  pl.BlockSpec((1, GLA_QK), lambda b, s: (0, 0)),
                  state_spec],
        out_specs=(pl.BlockSpec((rows, GLA_VW), lambda b, s: (rowblk(b, s), 0)), state_spec),
        scratch_shapes=[pltpu.VMEM((GLA_HEADS, GLA_DK, GLA_DV), F32)],
        compiler_params=_cparams(("parallel", "arbitrary")),
        name="gla_bwd" if rev else "gla_fwd",
    )(a, y, y, y, w_alpha_pad, b_alpha, s0)


def _rope(x, cos_t, sin_a, sin_b):
    return x * cos_t + pltpu.roll(x, LANE - ROPE_P, 1) * sin_a + pltpu.roll(x, ROPE_P, 1) * sin_b


def _lambda(lam_ref, lam_init):
    lq1, lk1, lq2, lk2 = lam_ref[0:1], lam_ref[1:2], lam_ref[2:3], lam_ref[3:4]
    return (jnp.exp(jnp.sum(lq1 * lk1, axis=-1, keepdims=True))
            - jnp.exp(jnp.sum(lq2 * lk2, axis=-1, keepdims=True)) + lam_init)


ATTN_SUB = 128


def _attn_core(q, k, v, lam, g_ref, o_ref, lam_init):
    tq = q.shape[0]
    sub = min(tq, ATTN_SUB)
    lane = lax.broadcasted_iota(jnp.int32, (sub, LANE), 1)
    blocks = [slice(i * sub, (i + 1) * sub) for i in range(tq // sub)]
    s2s = []
    for rows in blocks:
        qs = q[rows] * (DIFF_DH ** -0.5 * math.log2(math.e))
        q2 = jnp.concatenate([jnp.where(lane < DIFF_DH, qs, 0.0), jnp.where(lane >= DIFF_DH, qs, 0.0)],
                             axis=0).astype(BF16)
        s2s.append(lax.dot_general(q2, k, (((1,), (1,)), ((), ())), preferred_element_type=F32))
    ps = [jnp.exp2(s2 - jnp.max(s2, axis=-1, keepdims=True)) for s2 in s2s]
    rs = [1.0 / jnp.sum(p, axis=-1, keepdims=True) for p in ps]
    pvs = [jnp.dot(p.astype(BF16), v, preferred_element_type=F32) for p in ps]
    for rows, pv, r in zip(blocks, pvs, rs):
        pv = pv * r
        o = pv[:sub] - lam * pv[sub:]
        ms = jnp.mean(o * o, axis=-1, keepdims=True)
        o_ref[rows, :] = (o * lax.rsqrt(ms + EPS) * g_ref[...] * (1.0 - lam_init)).astype(o_ref.dtype)


def _attn_ctx_kernel(q_ref, k_ref, v_ref, lam_ref, g_ref, o_ref, *, lam_init):
    lam = _lambda(lam_ref, lam_init)
    _attn_core(q_ref[...].astype(F32), k_ref[...].astype(BF16), v_ref[...].astype(BF16), lam, g_ref, o_ref, lam_init)


def _attn_lat_kernel(q_ref, k_ref, v_ref, kc_ref, vc_ref, cq_ref, saq_ref, sbq_ref, ck_ref, sak_ref, sbk_ref,
                     lam_ref, g_ref, o_ref, k_scr, v_scr, *, seq, lam_init):
    @pl.when(pl.program_id(2) == 0)
    def _():
        k_scr[0:seq] = _rope(k_ref[...].astype(F32), ck_ref[...], sak_ref[...], sbk_ref[...]).astype(BF16)
        k_scr[seq:] = kc_ref[...].astype(BF16)
        v_scr[0:seq] = v_ref[...].astype(BF16)
        v_scr[seq:] = vc_ref[...].astype(BF16)

    lam = _lambda(lam_ref, lam_init)
    q = _rope(q_ref[...].astype(F32), cq_ref[...], saq_ref[...], sbq_ref[...])
    _attn_core(q, k_scr[...], v_scr[...], lam, g_ref, o_ref, lam_init)


def _diff_attn_ctx(y, lam_vecs, subln_g, batch, seq, lam_init):
    n = y.shape[0]
    tq = seq
    hb = lambda col: col // LANE
    return pl.pallas_call(
        functools.partial(_attn_ctx_kernel, lam_init=lam_init),
        out_shape=jax.ShapeDtypeStruct((n, DIFF_HEADS * LANE), BF16),
        grid=(batch, DIFF_HEADS),
        in_specs=[pl.BlockSpec((tq, LANE), lambda b, h: (b, hb(COL_QD) + h)),
                  pl.BlockSpec((seq, LANE), lambda b, h: (b, hb(COL_KD) + h)),
                  pl.BlockSpec((seq, LANE), lambda b, h: (b, hb(COL_VD) + h)),
                  pl.BlockSpec((4, DIFF_DH), lambda b, h: (0, 0)),
                  pl.BlockSpec((1, LANE), lambda b, h: (0, 0))],
        out_specs=pl.BlockSpec((tq, LANE), lambda b, h: (b, h)),
        compiler_params=_cparams(("parallel", "parallel")),
        name="diff_attn_ctx",
    )(y, y, y, lam_vecs, subln_g)


def _diff_attn_lat(y, k_cache, v_cache, tabs, lam_vecs, subln_g, batch, seq, past, tq, lam_init):
    n = y.shape[0]
    nq = seq // tq
    hb = lambda col: col // LANE
    cos_t, sin_a, sin_b = tabs
    qtab = pl.BlockSpec((tq, LANE), lambda b, h, i: (i, 0))
    ktab = pl.BlockSpec((seq, LANE), lambda b, h, i: (0, 0))
    return pl.pallas_call(
        functools.partial(_attn_lat_kernel, seq=seq, lam_init=lam_init),
        out_shape=jax.ShapeDtypeStruct((n, DIFF_HEADS * LANE), BF16),
        grid=(batch, DIFF_HEADS, nq),
        in_specs=[pl.BlockSpec((tq, LANE), lambda b, h, i: (b * nq + i, hb(COL_QD) + h)),
                  pl.BlockSpec((seq, LANE), lambda b, h, i: (b, hb(COL_KD) + h)),
                  pl.BlockSpec((seq, LANE), lambda b, h, i: (b, hb(COL_VD) + h)),
                  pl.BlockSpec((past, LANE), lambda b, h, i: (b, h)),
                  pl.BlockSpec((past, LANE), lambda b, h, i: (b, h)),
                  qtab, qtab, qtab, ktab, ktab, ktab,
                  pl.BlockSpec((4, DIFF_DH), lambda b, h, i: (0, 0)),
                  pl.BlockSpec((1, LANE), lambda b, h, i: (0, 0))],
        out_specs=pl.BlockSpec((tq, LANE), lambda b, h, i: (b * nq + i, h)),
        scratch_shapes=[pltpu.VMEM((seq + past, LANE), BF16), pltpu.VMEM((seq + past, LANE), BF16)],
        compiler_params=_cparams(("parallel", "parallel", "arbitrary")),
        name="diff_attn_lat",
    )(y, y, y, k_cache, v_cache, cos_t, sin_a, sin_b, cos_t, sin_a, sin_b, lam_vecs, subln_g)


def _rope_tables(seq):
    rows = seq // GRID_W
    pos_r = jnp.repeat(jnp.arange(rows, dtype=F32), GRID_W)
    pos_c = jnp.tile(jnp.arange(GRID_W, dtype=F32), rows)
    inv = ROPE_BASE ** (-jnp.arange(ROPE_P, dtype=F32) / ROPE_P)
    ang_r = pos_r[:, None] * inv
    ang_c = pos_c[:, None] * inv
    zeros = jnp.zeros((seq, ROPE_P), F32)

    def axis_tabs(ang):
        cs, sn = jnp.cos(ang), jnp.sin(ang)
        return (jnp.concatenate([cs, cs], -1), jnp.concatenate([-sn, zeros], -1), jnp.concatenate([zeros, sn], -1))

    tr, tc = axis_tabs(ang_r), axis_tabs(ang_c)
    reps = LANE // (2 * ROPE_AXIS_DIM)
    return tuple(jnp.tile(jnp.concatenate([a, b], -1), (1, reps)) for a, b in zip(tr, tc))


MERGE_SUB = 128


def _merge_kernel(of_ref, ob_ref, ra_ref, od_ref, ga_ref, gb_ref, x_ref, mod_ref, gn_ref, wpa_ref, wpb_ref,
                  wout_ref, l1g_ref, l1b_ref, wr_ref, x1_ref, h2_ref, aff_ref, *, alpha):
    mod = mod_ref[0]
    tm = x_ref.shape[0]
    sub = min(tm, MERGE_SUB)
    blocks = [slice(i * sub, (i + 1) * sub) for i in range(tm // sub)]

    def gla_out(rows):
        o = of_ref[rows, :] + ob_ref[rows, :]
        parts = []
        for h in range(GLA_HEADS):
            oh = o[:, h * GLA_DV:(h + 1) * GLA_DV]
            parts.append(oh * lax.rsqrt(jnp.mean(oh * oh, axis=-1, keepdims=True) + EPS) * gn_ref[...])
        return (jnp.concatenate(parts, axis=-1) * _silu(ra_ref[rows, :].astype(F32))).astype(BF16)

    o_as = [gla_out(rows) for rows in blocks]
    pas = [jnp.dot(o_a, wpa_ref[...], preferred_element_type=F32) for o_a in o_as]
    pbs = [jnp.dot(od_ref[rows, :], wpb_ref[...], preferred_element_type=F32) for rows in blocks]
    mergeds = [(_sigmoid(ga_ref[rows, :].astype(F32)) * pa + _sigmoid(gb_ref[rows, :].astype(F32)) * pb).astype(BF16)
               for rows, pa, pb in zip(blocks, pas, pbs)]
    mixes = [jnp.dot(merged, wout_ref[...], preferred_element_type=F32) for merged in mergeds]
    for rows, mix in zip(blocks, mixes):
        x1 = _ln_plain(alpha * x_ref[rows, :] + mod[2:3] * mix) * l1g_ref[...] + l1b_ref[...]
        x1_ref[rows, :] = x1
        h2 = _ln_plain(x1) * (1.0 + mod[4:5]) + mod[3:4]
        h2_ref[rows, :] = h2.astype(h2_ref.dtype)
        logits = lax.dot_general(wr_ref[...], h2, (((1,), (1,)), ((), ())), precision=HIGHEST,
                                 preferred_element_type=F32)
        e = jnp.exp(logits - jnp.max(logits, axis=0, keepdims=True))
        aff_ref[:, rows] = e / jnp.sum(e, axis=0, keepdims=True)


def _merge(o_f, o_b, y, o_d, x, mod_all, mod_row, gn, wpa, wpb, wout, l1g, l1b, wr_t, tm, alpha):
    n, d = x.shape
    cb = lambda col: col // d
    rowspec = lambda col: pl.BlockSpec((tm, d), lambda i: (i, col))
    const = lambda shape: pl.BlockSpec(shape, lambda i: (0,) * len(shape))
    return pl.pallas_call(
        functools.partial(_merge_kernel, alpha=alpha),
        out_shape=(jax.ShapeDtypeStruct((n, d), F32), jax.ShapeDtypeStruct((n, d), F32),
                   jax.ShapeDtypeStruct((N_EXPERTS, n), F32)),
        grid=(n // tm,),
        in_specs=[rowspec(0), rowspec(0),
                  rowspec(cb(COL_RA)), rowspec(0), rowspec(cb(COL_GA)), rowspec(cb(COL_GB)), rowspec(0),
                  pl.BlockSpec((1, 6, d), lambda i: (mod_row(i, tm), 0, 0)),
                  const((1, GLA_DV)), const((d, d)), const((d, d)), const((d, d)),
                  const((1, d)), const((1, d)), const((N_EXPERTS, d))],
        out_specs=(pl.BlockSpec((tm, d), lambda i: (i, 0)), pl.BlockSpec((tm, d), lambda i: (i, 0)),
                   pl.BlockSpec((N_EXPERTS, tm), lambda i: (0, i))),
        compiler_params=_cparams(("parallel",)),
        name="merge",
    )(o_f, o_b, y, o_d, y, y, x, mod_all, gn, wpa, wpb, wout, l1g, l1b, wr_t)


TM_ID, TM_ONE, TM_GHI, TM_GMID, TM_GLO, TM_POS = 0, 16, 32, 48, 64, 80
EXP_BISECT = 7
LIN_BISECT = 50


def _route_kernel(aff_ref, bind_ref, tm_ref, pos_ref, offb_ref, *, cap):
    n_exp, n = aff_ref.shape
    nb = n // LANE
    aff = aff_ref[...]
    capf = float(cap)

    def count_ge(thr):
        return jnp.sum(jnp.where(aff >= thr, 1.0, 0.0), axis=1, keepdims=True)

    def exp_step(_, c):
        klo, khi = c
        kmid = jnp.floor((klo + khi) * 0.5)
        ok = count_ge(jnp.exp2(kmid)) >= capf
        return jnp.where(ok, kmid, klo), jnp.where(ok, khi, kmid)

    klo, khi = lax.fori_loop(0, EXP_BISECT, exp_step,
                             (jnp.full((n_exp, 1), -127.0, F32), jnp.full((n_exp, 1), 1.0, F32)))
    lo0 = jnp.where(klo <= -127.0, 0.0, jnp.exp2(klo))
    hi0 = jnp.exp2(khi)

    def lin_step(_, c):
        lo, hi = c
        mid = lo + (hi - lo) * 0.5
        ok = count_ge(mid) >= capf
        return jnp.where(ok, mid, lo), jnp.where(ok, hi, mid)

    lo, hi = lax.fori_loop(0, LIN_BISECT, lin_step, (lo0, hi0))

    row = lax.broadcasted_iota(jnp.int32, (LANE, LANE), 0)
    col = lax.broadcasted_iota(jnp.int32, (LANE, LANE), 1)
    incl = jnp.where(row <= col, 1.0, 0.0).astype(BF16)
    excl = jnp.where(row < col, 1.0, 0.0).astype(BF16)
    bind = bind_ref[...]

    def prefix(mask):
        xb = jnp.where(mask, 1.0, 0.0).astype(BF16)
        stacked = jnp.concatenate([xb[:, j * LANE:(j + 1) * LANE] for j in range(nb)], axis=0)
        within = jnp.dot(stacked, incl, preferred_element_type=F32)
        within = jnp.concatenate([within[j * n_exp:(j + 1) * n_exp, :] for j in range(nb)], axis=1)
        block_tot = jnp.dot(xb, bind, preferred_element_type=F32)
        block_off = jnp.dot(block_tot.astype(BF16), excl, preferred_element_type=F32)
        off_hi = jnp.floor(block_off * (1.0 / LANE))
        off_lo = block_off - off_hi * LANE
        nt_dims = (((1,), (1,)), ((), ()))
        spread = (lax.dot_general(off_hi.astype(BF16), bind, nt_dims, preferred_element_type=F32) * LANE
                  + lax.dot_general(off_lo.astype(BF16), bind, nt_dims, preferred_element_type=F32))
        return within + spread, block_off

    above = aff >= hi
    tied = jnp.logical_and(aff >= lo, jnp.logical_not(above))
    need = capf - jnp.sum(jnp.where(above, 1.0, 0.0), axis=1, keepdims=True)
    tie_rank, _ = prefix(tied)
    sel = jnp.logical_or(above, jnp.logical_and(tied, tie_rank <= need))
    cum, block_off = prefix(sel)
    pos = jnp.where(sel, cum - 1.0, -1.0)
    pos_ref[...] = pos
    offb_ref[...] = block_off

    g_hi = aff.astype(BF16).astype(F32)
    rem = aff - g_hi
    g_mid = rem.astype(BF16).astype(F32)
    g_lo = (rem - g_mid).astype(BF16).astype(F32)
    lane_id = lax.broadcasted_iota(jnp.int32, (n_exp, LANE), 1).astype(F32)
    ones = jnp.ones((n_exp, LANE), F32)
    fill = jnp.zeros((LANE - 6 * n_exp, LANE), F32)
    for j in range(nb):
        cs = slice(j * LANE, (j + 1) * LANE)
        local = lane_id + float(LANE * (j % (COMBINE_TM // LANE)))
        fields = jnp.concatenate([local, ones, g_hi[:, cs], g_mid[:, cs], g_lo[:, cs], pos[:, cs], fill], axis=0)
        tm_ref[cs, :] = jnp.transpose(fields)


def _route(aff_t, cap):
    n_exp, n = aff_t.shape
    bind = (jnp.arange(n, dtype=jnp.int32)[:, None] // LANE == jnp.arange(LANE, dtype=jnp.int32)[None, :]).astype(BF16)
    return pl.pallas_call(
        functools.partial(_route_kernel, cap=cap),
        out_shape=(jax.ShapeDtypeStruct((n, LANE), F32), jax.ShapeDtypeStruct((n_exp, n), F32),
                   jax.ShapeDtypeStruct((n_exp, LANE), F32)),
        compiler_params=pltpu.CompilerParams(vmem_limit_bytes=VMEM_LIMIT),
        name="route",
    )(aff_t, bind)


def _compact_kernel(base_ref, npass_ref, pos_ref, tm_ref, acc_ref, *, cap):
    t = pl.program_id(0)
    nt = pl.num_programs(0)
    n_exp = pos_ref.shape[0]

    @pl.when(t == 0)
    def _():
        acc_ref[...] = jnp.zeros(acc_ref.shape, F32)

    pos = pos_ref[...]
    table = tm_ref[...].astype(BF16)
    lane = lax.broadcasted_iota(jnp.int32, (1, LANE), 1)
    tile_scale = jnp.where(jnp.logical_and(lane >= TM_ONE, lane < TM_GHI), t.astype(F32), 1.0)
    slot_iota = lax.broadcasted_iota(jnp.int32, (WIN, pos.shape[1]), 0).astype(F32)

    def one_pass(p, carry):
        onehot = jnp.concatenate(
            [jnp.where(pos[e:e + 1, :] - (base_ref[e * nt + t] + p * WIN).astype(F32) == slot_iota, 1.0, 0.0)
             for e in range(n_exp)], axis=0).astype(BF16)
        rec = jnp.dot(onehot, table, preferred_element_type=F32) * tile_scale
        for e in range(n_exp):
            start = pl.multiple_of(jnp.minimum(base_ref[e * nt + t] + p * WIN, cap), 8)
            mine = jnp.logical_and(lane % n_exp == e, lane < TM_POS)
            acc_ref[pl.ds(start, WIN), :] += jnp.where(mine, rec[e * WIN:(e + 1) * WIN, :], 0.0)
        return carry

    lax.fori_loop(0, npass_ref[t], one_pass, 0)


def _compact(pos, tm, base, npass, cap):
    n_exp, n = pos.shape
    tm_rows = COMBINE_TM
    grid_spec = pltpu.PrefetchScalarGridSpec(
        num_scalar_prefetch=2,
        grid=(n // tm_rows,),
        in_specs=[pl.BlockSpec((n_exp, tm_rows), lambda i, *_: (0, i)),
                  pl.BlockSpec((tm_rows, LANE), lambda i, *_: (i, 0))],
        out_specs=pl.BlockSpec((cap + WIN, LANE), lambda i, *_: (0, 0)))
    return pl.pallas_call(
        functools.partial(_compact_kernel, cap=cap),
        out_shape=jax.ShapeDtypeStruct((cap + WIN, LANE), F32),
        grid_spec=grid_spec,
        compiler_params=_cparams(("arbitrary",)),
        name="compact",
    )(base.reshape(-1), npass, pos, tm)


def _routing(aff_t, cap):
    n_exp, n = aff_t.shape
    tm, pos, offb = _route(aff_t, cap)
    blocks_per_tile = COMBINE_TM // LANE
    off = offb[:, :n // LANE].astype(jnp.int32)[:, ::blocks_per_tile]
    nxt = jnp.concatenate([off[:, 1:], jnp.full((n_exp, 1), cap, jnp.int32)], axis=1)
    base = (off // 8) * 8
    npass = jnp.maximum(jnp.max(-(-(nxt - base) // WIN), axis=0), 1).astype(jnp.int32)
    acc = _compact(pos, tm, base, npass, cap)[:cap]
    idx = (acc[:, TM_ID:TM_ID + n_exp] + COMBINE_TM * acc[:, TM_ONE:TM_ONE + n_exp]).T.astype(jnp.int32)
    gate = (acc[:, TM_GHI:TM_GHI + n_exp] + acc[:, TM_GMID:TM_GMID + n_exp] + acc[:, TM_GLO:TM_GLO + n_exp]).T
    return idx, gate, tm, base, npass


def _moe_kernel(idx_ref, hp_hbm, hs_hbm, gate_ref, wg_ref, wu_ref, wd_ref, yp_ref, ys_ref, xf_scr, xb_scr, sem,
                *, rows, cap_p, nf, s_tiles):
    e = pl.program_id(0)
    f = pl.program_id(1)
    n_exp = pl.num_programs(0)
    cap = xb_scr.shape[0]
    cap_s = cap - cap_p
    total = xf_scr.shape[0]

    def wait_rows(count):
        def drain(r, carry):
            pltpu.make_async_copy(hp_hbm.at[pl.ds(0, 1)], xf_scr.at[pl.ds(r, 1)], sem).wait()
            return carry
        lax.fori_loop(0, count, drain, 0, unroll=8)

    @pl.when(f == 0)
    def _():
        @pl.when(e == 0)
        def _():
            def issue_from(h_hbm):
                def issue(r, carry):
                    pltpu.make_async_copy(h_hbm.at[pl.ds(idx_ref[r], 1)], xf_scr.at[pl.ds(r, 1)], sem).start()
                    return carry
                return issue

            lax.fori_loop(0, cap_p, issue_from(hp_hbm), 0, unroll=8)
            lax.fori_loop(cap_p, cap, issue_from(hs_hbm), 0, unroll=8)
            wait_rows(cap)

        @pl.when(e > 0)
        def _():
            wait_rows(total)

        xb_scr[...] = xf_scr[0:cap, :].astype(BF16)

    nxt = jnp.minimum(e + 1, n_exp - 1) * cap

    def prefetch_group(h_hbm, group, n_groups, first_slot, spare_slot):
        real = group < n_groups
        src = first_slot + jnp.where(real, group, 0) * GATHER_G
        dst = jnp.where(real, first_slot + group * GATHER_G, spare_slot + (group - n_groups) * GATHER_G)
        for k in range(GATHER_G):
            pltpu.make_async_copy(h_hbm.at[pl.ds(idx_ref[nxt + src + k], 1)], xf_scr.at[pl.ds(dst + k, 1)],
                                  sem).start()

    p_groups, s_groups = cap_p // GATHER_G, cap_s // GATHER_G
    wg = wg_ref[0].astype(BF16)
    wu = wu_ref[0].astype(BF16)
    wd = wd_ref[0].astype(BF16)
    for t in range(cap // rows):
        if t == 0:
            prefetch_group(hp_hbm, f, p_groups, 0, cap)
        elif t <= s_tiles:
            prefetch_group(hs_hbm, f * s_tiles + (t - 1), s_groups, cap_p, cap + (nf - p_groups) * GATHER_G)
        lo = t * rows
        out_ref, olo = (yp_ref, lo) if lo < cap_p else (ys_ref, lo - cap_p)
        xr = xb_scr[lo:lo + rows, :]
        hg = jnp.dot(xr, wg, preferred_element_type=F32)
        hu = jnp.dot(xr, wu, preferred_element_type=F32)
        hid = (_silu(hg) * hu).astype(BF16)
        prev = jnp.where(f > 0, out_ref[0, olo:olo + rows, :], 0.0)
        out_ref[0, olo:olo + rows, :] = prev + jnp.dot(hid, wd, preferred_element_type=F32)

    @pl.when(jnp.logical_and(e == n_exp - 1, f == nf - 1))
    def _():
        wait_rows(total)

    @pl.when(f == nf - 1)
    def _():
        yp_ref[0, 0:cap_p, :] = yp_ref[0, 0:cap_p, :] * gate_ref[0, 0:cap_p, :]
        ys_ref[0, 0:cap_s, :] = ys_ref[0, 0:cap_s, :] * gate_ref[0, cap_p:, :]
        yp_ref[0, cap_p:, :] = jnp.zeros((WIN, yp_ref.shape[2]), F32)
        ys_ref[0, cap_s:, :] = jnp.zeros((WIN, ys_ref.shape[2]), F32)


def _moe(idx, hp, hs, gate, wg, wu, wd, cap_p, ft, rows):
    e, cap = idx.shape
    cap_s = cap - cap_p
    d = hp.shape[1]
    ff = wg.shape[2]
    nf = ff // ft
    p_groups, s_groups = cap_p // GATHER_G, cap_s // GATHER_G
    s_tiles = -(-s_groups // nf)
    assert cap_p % GATHER_G == 0 and cap_s % GATHER_G == 0 and p_groups <= nf and s_tiles <= cap // rows - 1
    gather_rows = nf * (1 + s_tiles) * GATHER_G
    grid_spec = pltpu.PrefetchScalarGridSpec(
        num_scalar_prefetch=1,
        grid=(e, ff // ft),
        in_specs=[pl.BlockSpec(memory_space=pl.ANY),
                  pl.BlockSpec(memory_space=pl.ANY),
                  pl.BlockSpec((1, cap, 1), lambda i, f, idx_ref: (i, 0, 0)),
                  pl.BlockSpec((1, d, ft), lambda i, f, idx_ref: (i, 0, f)),
                  pl.BlockSpec((1, d, ft), lambda i, f, idx_ref: (i, 0, f)),
                  pl.BlockSpec((1, ft, d), lambda i, f, idx_ref: (i, f, 0))],
        out_specs=(pl.BlockSpec((1, cap_p + WIN, d), lambda i, f, idx_ref: (i, 0, 0)),
                   pl.BlockSpec((1, cap_s + WIN, d), lambda i, f, idx_ref: (i, 0, 0))),
        scratch_shapes=[pltpu.VMEM((gather_rows, d), F32), pltpu.VMEM((cap, d), BF16), pltpu.SemaphoreType.DMA])
    return pl.pallas_call(
        functools.partial(_moe_kernel, rows=rows, cap_p=cap_p, nf=nf, s_tiles=s_tiles),
        out_shape=(jax.ShapeDtypeStruct((e, cap_p + WIN, d), F32), jax.ShapeDtypeStruct((e, cap_s + WIN, d), F32)),
        grid_spec=grid_spec,
        compiler_params=_cparams(("arbitrary", "arbitrary")),
        name="moe",
    )(idx.reshape(-1), hp, hs, gate, wg, wu, wd)


def _combine_kernel(base_ref, npass_ref, x1_ref, tm_ref, mod_ref, g_ref, b_ref, ye_hbm, o_ref,
                    ybuf, xbuf, sem, xsem, *, alpha):
    t = pl.program_id(0)
    nt = pl.num_programs(0)
    n_exp = ye_hbm.shape[0]
    tm = x1_ref.shape[0]
    cap = ye_hbm.shape[1] - WIN

    def window(tile, e, p, dst, s):
        start = pl.multiple_of(jnp.minimum(base_ref[e * nt + tile] + p * WIN, cap), 8)
        return pltpu.make_async_copy(ye_hbm.at[e, pl.ds(start, WIN), :], dst.at[pl.ds(e * WIN, WIN), :], s)

    def start_first_pass(tile, slot):
        for e in range(n_exp):
            window(tile, e, 0, ybuf.at[slot], sem.at[slot]).start()

    def expand(yw, p):
        lane = lax.broadcasted_iota(jnp.int32, (tm, LANE), 1).astype(F32)
        groups = []
        for gi in range(n_exp * WIN // LANE):
            ea, eb = 2 * gi, 2 * gi + 1
            ra = tm_ref[:, TM_POS + ea:TM_POS + ea + 1] - (base_ref[ea * nt + t] + p * WIN).astype(F32)
            rb = tm_ref[:, TM_POS + eb:TM_POS + eb + 1] - (base_ref[eb * nt + t] + (p - 1) * WIN).astype(F32)
            groups.append(jnp.where(lane == jnp.where(lane < WIN, ra, rb), 1.0, 0.0).astype(BF16))
        onehot = jnp.concatenate(groups, axis=1)
        hi = yw.astype(BF16)
        lo = (yw - hi.astype(F32)).astype(BF16)
        return (jnp.dot(onehot, hi, preferred_element_type=F32) + jnp.dot(onehot, lo, preferred_element_type=F32))

    @pl.when(t == 0)
    def _():
        start_first_pass(0, 0)

    slot = t % 2
    for e in range(n_exp):
        window(t, e, 0, ybuf.at[slot], sem.at[slot]).wait()

    @pl.when(t + 1 < nt)
    def _():
        start_first_pass(t + 1, 1 - slot)

    npass = npass_ref[t]

    def start_extra_pass(p):
        for e in range(n_exp):
            window(t, e, p, xbuf.at[p % 2], xsem.at[p % 2]).start()

    @pl.when(npass > 1)
    def _():
        start_extra_pass(1)

    acc = expand(ybuf[slot], 0)

    def extra_pass(p, acc):
        for e in range(n_exp):
            window(t, e, p, xbuf.at[p % 2], xsem.at[p % 2]).wait()

        @pl.when(p + 1 < npass)
        def _():
            start_extra_pass(p + 1)

        return acc + expand(xbuf[p % 2], p)

    acc = lax.fori_loop(1, npass, extra_pass, acc)
    mod = mod_ref[0]
    o_ref[...] = _ln_plain(alpha * x1_ref[...] + mod[5:6] * acc) * g_ref[...] + b_ref[...]


def _combine(x1, table, base, npass, ye, mod_all, mod_row, g, b, alpha):
    n, d = x1.shape
    n_exp = ye.shape[0]
    tm = COMBINE_TM
    grid_spec = pltpu.PrefetchScalarGridSpec(
        num_scalar_prefetch=2,
        grid=(n // tm,),
        in_specs=[pl.BlockSpec((tm, d), lambda i, *_: (i, 0)),
                  pl.BlockSpec((tm, LANE), lambda i, *_: (i, 0)),
                  pl.BlockSpec((1, 6, d), lambda i, *_: (mod_row(i, tm), 0, 0)),
                  pl.BlockSpec((1, d), lambda i, *_: (0, 0)), pl.BlockSpec((1, d), lambda i, *_: (0, 0)),
                  pl.BlockSpec(memory_space=pl.ANY)],
        out_specs=pl.BlockSpec((tm, d), lambda i, *_: (i, 0)),
        scratch_shapes=[pltpu.VMEM((2, n_exp * WIN, d), F32), pltpu.VMEM((2, n_exp * WIN, d), F32),
                        pltpu.SemaphoreType.DMA((2,)), pltpu.SemaphoreType.DMA((2,))])
    return pl.pallas_call(
        functools.partial(_combine_kernel, alpha=alpha),
        out_shape=jax.ShapeDtypeStruct((n, d), F32),
        grid_spec=grid_spec,
        compiler_params=_cparams(("arbitrary",)),
        name="combine",
    )(base.reshape(-1), npass, x1, table, mod_all, g, b, ye)


def _pick(n, pref):
    t = pref
    while n % t:
        t //= 2
    return t


def _mix_group(x3, mod_all, mod_row, lw, lam_init, alpha, s0, ctx):
    batch, seq, d = x3.shape
    n = batch * seq
    x = x3.reshape(n, d)
    tile_span = n if ctx is None else seq
    y_dtype = F32 if ctx is None else BF16
    y, a = _in_proj(x, mod_all, mod_row, lw["w_main"], lw["w_a"], _pick(tile_span, 1024), 1024, y_dtype)
    o_f, s_f = _gla(y, a, lw["w_alpha_pad"][0], lw["b_alpha"][0], s0[0], batch, seq, False)
    o_b, s_b = _gla(y, a, lw["w_alpha_pad"][1], lw["b_alpha"][1], s0[1], batch, seq, True)
    if ctx is None:
        o_d = _diff_attn_ctx(y, lw["lam_vecs"], lw["subln_g"], batch, seq, lam_init)
    else:
        k_cache, v_cache = ctx
        past = k_cache.shape[0] // batch
        o_d = _diff_attn_lat(y, k_cache, v_cache, _rope_tables(seq), lw["lam_vecs"], lw["subln_g"],
                             batch, seq, past, _pick(seq, 2048), lam_init)
    tm = _pick(seq, 512)
    x1, h2, aff_t = _merge(o_f, o_b, y, o_d, x, mod_all, mod_row, lw["gla_norm_g"], lw["w_proj_a"], lw["w_proj_b"],
                           lw["w_out"], lw["ln1_g"], lw["ln1_b"], lw["w_router_t"], tm, alpha)
    idx, gate, table, base, npass = _routing(aff_t, CAPACITY_FACTOR * n // N_EXPERTS)
    return dict(x1=x1, h2=h2, gate=gate, idx=idx, table=table, base=base, npass=npass, y=y, states=(s_f, s_b),
                mod_row=mod_row, shape=(batch, seq, d))


def _ffn_groups(gp, gs, mod_all, lw, alpha):
    cap_p, cap_s = gp["idx"].shape[1], gs["idx"].shape[1]
    idx = jnp.concatenate([gp["idx"], gs["idx"]], axis=1)
    gate = jnp.concatenate([gp["gate"], gs["gate"]], axis=1)[..., None]
    ye_p, ye_s = _moe(idx, gp["h2"], gs["h2"], gate, lw["w_exp_gate"], lw["w_exp_up"], lw["w_exp_down"],
                      cap_p, 256, _pick(math.gcd(cap_p, cap_s), 512))
    outs = []
    for g, ye in ((gp, ye_p), (gs, ye_s)):
        batch, seq, d = g["shape"]
        x2 = _combine(g["x1"], g["table"], g["base"], g["npass"], ye, mod_all, g["mod_row"],
                      lw["ln2_g"], lw["ln2_b"], alpha)
        outs.append(x2.reshape(batch, seq, d))
    return outs


def kernel(x_prompt, x_sample, cache_diff_k, cache_diff_v, state_gla_fwd, state_gla_bwd, c, c_ctx, w_mod, b_mod, w_in, w_alpha_f, b_alpha_f, w_alpha_b, b_alpha_b, gla_norm_g, w_proj_a, lambda_q1, lambda_k1, lambda_q2, lambda_k2, subln_g, w_proj_b, w_out, ln1_g, ln1_b, w_router, w_exp_gate, w_exp_up, w_exp_down, ln2_g, ln2_b):
    depth = w_in.shape[0]
    bp, sp, d = x_prompt.shape
    bs, ss, _ = x_sample.shape
    alpha = (2.0 * depth) ** 0.25
    xp, xs = x_prompt, x_sample

    n_mod = 1 + bs
    mod_rows = -(-n_mod // 8) * 8
    c_all = jnp.zeros((mod_rows, d), F32).at[0].set(c_ctx).at[1:n_mod].set(c)

    new_k, new_v, new_f, new_b = [], [], [], []
    for l in range(depth):
        lam_init = 0.8 - 0.6 * math.exp(-0.3 * l)
        mod_all = _modulation(c_all, w_mod[l], b_mod[l]).reshape(mod_rows, 6, d)
        wi = w_in[l]
        o = 0
        pieces = {}
        for name, width in zip(("qa", "ka", "va", "ra", "af", "ab", "qd", "kd", "vd", "ga", "gb"),
                               (GLA_QK, GLA_QK, GLA_VW, GLA_VW, GLA_RANK, GLA_RANK, d, d, d, d, d)):
            pieces[name] = wi[:, o:o + width]
            o += width
        w_main = jnp.concatenate([pieces[k] for k in ("qa", "ka", "va", "ra", "qd", "kd", "vd", "ga", "gb")],
                                 axis=1).astype(BF16)
        w_a = jnp.concatenate([pieces["af"], pieces["ab"], jnp.zeros((d, LANE - 2 * GLA_RANK), F32)],
                              axis=1).astype(BF16)
        w_alpha_pad = jnp.zeros((2, LANE, GLA_QK), F32)
        w_alpha_pad = w_alpha_pad.at[0, 0:GLA_RANK].set(w_alpha_f[l]).at[1, GLA_RANK:2 * GLA_RANK].set(w_alpha_b[l])
        lw = dict(
            w_main=w_main, w_a=w_a, w_alpha_pad=w_alpha_pad.astype(BF16),
            b_alpha=jnp.stack([b_alpha_f[l], b_alpha_b[l]])[:, None, :],
            lam_vecs=jnp.stack([lambda_q1[l], lambda_k1[l], lambda_q2[l], lambda_k2[l]]),
            subln_g=subln_g[l][None, :], gla_norm_g=gla_norm_g[l][None, :],
            w_proj_a=w_proj_a[l].astype(BF16), w_proj_b=w_proj_b[l].astype(BF16), w_out=w_out[l].astype(BF16),
            ln1_g=ln1_g[l][None, :], ln1_b=ln1_b[l][None, :], w_router_t=w_router[l].T,
            w_exp_gate=w_exp_gate[l], w_exp_up=w_exp_up[l], w_exp_down=w_exp_down[l],
            ln2_g=ln2_g[l][None, :], ln2_b=ln2_b[l][None, :])

        zeros = jnp.zeros((2, bp, GLA_HEADS, GLA_DK, GLA_DV), F32)
        gp = _mix_group(xp, mod_all, lambda i, tm: 0, lw, lam_init, alpha, zeros, None)
        y_p = gp["y"]
        new_k.append(y_p[:, COL_KD:COL_KD + d].reshape(bp, sp, DIFF_HEADS, 2, DIFF_DH))
        new_v.append(y_p[:, COL_VD:COL_VD + d].reshape(bp, sp, DIFF_HEADS, 2 * DIFF_DH))
        new_f.append(gp["states"][0])
        new_b.append(gp["states"][1])

        s0 = (state_gla_fwd[:, l], state_gla_bwd[:, l])
        past = cache_diff_k.shape[2]
        ctx = (cache_diff_k[:, l].reshape(bs * past, d), cache_diff_v[:, l].reshape(bs * past, d))
        gs = _mix_group(xs, mod_all, lambda i, tm: 1 + (i * tm) // ss, lw, lam_init, alpha, s0, ctx)
        xp, xs = _ffn_groups(gp, gs, mod_all, lw, alpha)

    return (xp, xs, jnp.stack(new_k, axis=1), jnp.stack(new_v, axis=1),
            jnp.stack(new_f, axis=1), jnp.stack(new_b, axis=1))
```

```python
import functools
import math

import jax
import jax.numpy as jnp
from jax import lax
from jax.experimental import pallas as pl
from jax.experimental.pallas import tpu as pltpu

F32 = jnp.float32
BF16 = jnp.bfloat16
HIGHEST = lax.Precision.HIGHEST

LANE = 128
EPS = 1e-6
GRID_W = 64
GLA_HEADS = 4
GLA_DK = 128
GLA_DV = 256
GLA_QK = GLA_HEADS * GLA_DK
GLA_VW = GLA_HEADS * GLA_DV
GLA_RANK = 16
GLA_TAU = 16.0
CHUNK = 64
DIFF_HEADS = 8
DIFF_DH = 64
ROPE_BASE = 10000.0
ROPE_AXIS_DIM = DIFF_DH // 2
ROPE_P = ROPE_AXIS_DIM // 2
N_EXPERTS = 16
CAPACITY_FACTOR = 2
VMEM_LIMIT = 56 * 1024 * 1024
COMBINE_TM = 256
WIN = 64
GATHER_G = 64

COL_QA, COL_KA, COL_VA, COL_RA, COL_QD, COL_KD, COL_VD, COL_GA, COL_GB = (
    0, 512, 1024, 2048, 3072, 4096, 5120, 6144, 7168)
MAIN_W = 8192


def _cparams(sem):
    return pltpu.CompilerParams(dimension_semantics=sem, vmem_limit_bytes=VMEM_LIMIT)


def _ln_plain(x):
    mu = jnp.mean(x, axis=-1, keepdims=True)
    xc = x - mu
    var = jnp.mean(xc * xc, axis=-1, keepdims=True)
    return xc * lax.rsqrt(var + EPS)


def _sigmoid(x):
    return 1.0 / (1.0 + jnp.exp(-x))


def _silu(x):
    return x * _sigmoid(x)


def _mod_kernel(c_ref, w_ref, b_ref, o_ref):
    c = c_ref[...]
    o_ref[...] = jnp.dot(_silu(c), w_ref[...], precision=HIGHEST, preferred_element_type=F32) + b_ref[...]


def _modulation(c_all, w_mod, b_mod):
    rows, d = c_all.shape
    n = w_mod.shape[1]
    tn = 1024
    return pl.pallas_call(
        _mod_kernel,
        out_shape=jax.ShapeDtypeStruct((rows, n), F32),
        grid=(n // tn,),
        in_specs=[pl.BlockSpec((rows, d), lambda j: (0, 0)),
                  pl.BlockSpec((d, tn), lambda j: (0, j)),
                  pl.BlockSpec((1, tn), lambda j: (0, j))],
        out_specs=pl.BlockSpec((rows, tn), lambda j: (0, j)),
        compiler_params=_cparams(("parallel",)),
        name="modulation",
    )(c_all, w_mod, b_mod.reshape(1, n))


def _in_proj_kernel(x_ref, mod_ref, w_ref, wa_ref, y_ref, a_ref, h_scr):
    @pl.when(pl.program_id(1) == 0)
    def _():
        mod = mod_ref[0]
        h = _ln_plain(x_ref[...]) * (1.0 + mod[1:2]) + mod[0:1]
        hb = h.astype(BF16)
        h_scr[...] = hb
        a_ref[...] = jnp.dot(hb, wa_ref[...], preferred_element_type=F32)

    y_ref[...] = jnp.dot(h_scr[...], w_ref[...], preferred_element_type=F32).astype(y_ref.dtype)


def _in_proj(x, mod_all, mod_row, w_main, w_a, tm, tn, y_dtype):
    n, d = x.shape
    nw = w_main.shape[1]
    return pl.pallas_call(
        _in_proj_kernel,
        out_shape=(jax.ShapeDtypeStruct((n, nw), y_dtype), jax.ShapeDtypeStruct((n, LANE), F32)),
        grid=(n // tm, nw // tn),
        in_specs=[pl.BlockSpec((tm, d), lambda i, j: (i, 0)),
                  pl.BlockSpec((1, 6, d), lambda i, j: (mod_row(i, tm), 0, 0)),
                  pl.BlockSpec((d, tn), lambda i, j: (0, j)),
                  pl.BlockSpec((d, LANE), lambda i, j: (0, 0))],
        out_specs=(pl.BlockSpec((tm, tn), lambda i, j: (i, j)),
                   pl.BlockSpec((tm, LANE), lambda i, j: (i, 0))),
        scratch_shapes=[pltpu.VMEM((tm, d), BF16)],
        compiler_params=_cparams(("parallel", "arbitrary")),
        name="in_proj",
    )(x, mod_all, w_main, w_a)


def _gla_kernel(a_ref, q_ref, k_ref, v_ref, wal_ref, bal_ref, s0_ref, o_ref, sout_ref, s_scr, *, rev, nch):
    step = pl.program_id(1)

    @pl.when(step == 0)
    def _():
        s_scr[...] = s0_ref[0]

    z = jnp.dot(a_ref[...].astype(BF16), wal_ref[...], preferred_element_type=F32) + bal_ref[...]
    g = (jnp.minimum(z, 0.0) - jnp.log(1.0 + jnp.exp(-jnp.abs(z)))) * (1.0 / GLA_TAU)
    row = lax.broadcasted_iota(jnp.int32, (CHUNK, CHUNK), 0)
    col = lax.broadcasted_iota(jnp.int32, (CHUNK, CHUNK), 1)
    keep = (col >= row) if rev else (col <= row)
    tri = keep.astype(F32)
    chunks = list(reversed(range(nch)) if rev else range(nch))
    heads = [(slice(h * GLA_DK, (h + 1) * GLA_DK), slice(h * GLA_DV, (h + 1) * GLA_DV)) for h in range(GLA_HEADS)]
    pre = {}
    for ci in chunks:
        rs = slice(ci * CHUNK, (ci + 1) * CHUNK)
        gc = g[rs]
        b = jnp.dot(tri, gc, precision=HIGHEST, preferred_element_type=F32)
        b_last = jnp.sum(gc, axis=0, keepdims=True)
        eb_last_t = jnp.transpose(jnp.broadcast_to(jnp.exp(b_last), (8, GLA_QK)))
        k = k_ref[rs, :].astype(F32)
        pre[ci] = dict(rs=rs, eb_last_t=eb_last_t,
                       qe=(q_ref[rs, :].astype(F32) * (GLA_DK ** -0.5) * jnp.exp(b)).astype(BF16),
                       ke=(k * jnp.exp(-b)).astype(BF16),
                       kd=(k * jnp.exp(b_last - b)).astype(BF16),
                       v=v_ref[rs, :].astype(BF16))
    for ci in chunks:
        c = pre[ci]
        c["a"] = [jnp.where(keep, lax.dot_general(c["qe"][:, ks], c["ke"][:, ks], (((1,), (1,)), ((), ())),
                                                  preferred_element_type=F32), 0.0).astype(BF16)
                  for ks, _ in heads]
    for ci in chunks:
        c = pre[ci]
        c["o"] = [jnp.dot(c["a"][h], c["v"][:, vs], preferred_element_type=F32) for h, (_, vs) in enumerate(heads)]
        c["upd"] = [lax.dot_general(c["kd"][:, ks], c["v"][:, vs], (((0,), (0,)), ((), ())),
                                    preferred_element_type=F32) for ks, vs in heads]
    state = [s_scr[h] for h in range(GLA_HEADS)]
    for ci in chunks:
        c = pre[ci]
        for h, (ks, vs) in enumerate(heads):
            o_ref[c["rs"], vs] = c["o"][h] + jnp.dot(c["qe"][:, ks], state[h].astype(BF16),
                                                     preferred_element_type=F32)
            state[h] = c["eb_last_t"][ks, 0:1] * state[h] + c["upd"][h]
    for h in range(GLA_HEADS):
        s_scr[h] = state[h]

    @pl.when(step == pl.num_programs(1) - 1)
    def _():
        sout_ref[0] = s_scr[...]


def _gla(y, a, w_alpha_pad, b_alpha, s0, batch, seq, rev):
    n = y.shape[0]
    nch = min(seq // CHUNK, 16)
    rows = nch * CHUNK
    ns = seq // rows

    def rowblk(b, s):
        return b * ns + (ns - 1 - s if rev else s)

    state_spec = pl.BlockSpec((1, GLA_HEADS, GLA_DK, GLA_DV), lambda b, s: (b, 0, 0, 0))
    return pl.pallas_call(
        functools.partial(_gla_kernel, rev=rev, nch=nch),
        out_shape=(jax.ShapeDtypeStruct((n, GLA_VW), F32),
                   jax.ShapeDtypeStruct((batch, GLA_HEADS, GLA_DK, GLA_DV), F32)),
        grid=(batch, ns),
        in_specs=[pl.BlockSpec((rows, LANE), lambda b, s: (rowblk(b, s), 0)),
                  pl.BlockSpec((rows, GLA_QK), lambda b, s: (rowblk(b, s), COL_QA // GLA_QK)),
                  pl.BlockSpec((rows, GLA_QK), lambda b, s: (rowblk(b, s), COL_KA // GLA_QK)),
                  pl.BlockSpec((rows, GLA_VW), lambda b, s: (rowblk(b, s), COL_VA // GLA_VW)),
                  pl.BlockSpec((LANE, GLA_QK), lambda b, s: (0, 0)),
                  pl.BlockSpec((1, GLA_QK), lambda b, s: (0, 0)),
                  state_spec],
        out_specs=(pl.BlockSpec((rows, GLA_VW), lambda b, s: (rowblk(b, s), 0)), state_spec),
        scratch_shapes=[pltpu.VMEM((GLA_HEADS, GLA_DK, GLA_DV), F32)],
        compiler_params=_cparams(("parallel", "arbitrary")),
        name="gla_bwd" if rev else "gla_fwd",
    )(a, y, y, y, w_alpha_pad, b_alpha, s0)


def _rope(x, cos_t, sin_a, sin_b):
    return x * cos_t + pltpu.roll(x, LANE - ROPE_P, 1) * sin_a + pltpu.roll(x, ROPE_P, 1) * sin_b


def _lambda(lam_ref, lam_init):
    lq1, lk1, lq2, lk2 = lam_ref[0:1], lam_ref[1:2], lam_ref[2:3], lam_ref[3:4]
    return (jnp.exp(jnp.sum(lq1 * lk1, axis=-1, keepdims=True))
            - jnp.exp(jnp.sum(lq2 * lk2, axis=-1, keepdims=True)) + lam_init)


ATTN_SUB = 128


def _attn_heads(heads, lam, g_ref, o_ref, lam_init):
    tq = heads[0][0].shape[0]
    sub = min(tq, ATTN_SUB)
    lane = lax.broadcasted_iota(jnp.int32, (sub, LANE), 1)
    work = [(q, k, v, cols, slice(i * sub, (i + 1) * sub)) for q, k, v, cols in heads for i in range(tq // sub)]
    s2s = []
    for q, k, _, _, rows in work:
        qs = q[rows] * (DIFF_DH ** -0.5 * math.log2(math.e))
        q2 = jnp.concatenate([jnp.where(lane < DIFF_DH, qs, 0.0), jnp.where(lane >= DIFF_DH, qs, 0.0)],
                             axis=0).astype(BF16)
        s2s.append(lax.dot_general(q2, k, (((1,), (1,)), ((), ())), preferred_element_type=F32))
    ps = [jnp.exp2(s2 - jnp.max(s2, axis=-1, keepdims=True)) for s2 in s2s]
    rs = [1.0 / jnp.sum(p, axis=-1, keepdims=True) for p in ps]
    pvs = [jnp.dot(p.astype(BF16), w[2], preferred_element_type=F32) for p, w in zip(ps, work)]
    for (_, _, _, cols, rows), pv, r in zip(work, pvs, rs):
        pv = pv * r
        o = pv[:sub] - lam * pv[sub:]
        ms = jnp.mean(o * o, axis=-1, keepdims=True)
        o_ref[rows, cols] = (o * lax.rsqrt(ms + EPS) * g_ref[...] * (1.0 - lam_init)).astype(o_ref.dtype)


def _attn_core(q, k, v, lam, g_ref, o_ref, lam_init):
    _attn_heads([(q, k, v, slice(0, LANE))], lam, g_ref, o_ref, lam_init)


def _attn_ctx_kernel(q_ref, k_ref, v_ref, lam_ref, g_ref, o_ref, *, lam_init):
    lam = _lambda(lam_ref, lam_init)
    heads = []
    for h in range(DIFF_HEADS):
        cols = slice(h * LANE, (h + 1) * LANE)
        heads.append((q_ref[:, cols].astype(F32), k_ref[:, cols].astype(BF16), v_ref[:, cols].astype(BF16), cols))
    _attn_heads(heads, lam, g_ref, o_ref, lam_init)


def _attn_lat_kernel(q_ref, k_ref, v_ref, kc_ref, vc_ref, cq_ref, saq_ref, sbq_ref, ck_ref, sak_ref, sbk_ref,
                     lam_ref, g_ref, o_ref, k_scr, v_scr, *, seq, lam_init):
    @pl.when(pl.program_id(2) == 0)
    def _():
        k_scr[0:seq] = _rope(k_ref[...].astype(F32), ck_ref[...], sak_ref[...], sbk_ref[...]).astype(BF16)
        k_scr[seq:] = kc_ref[...].astype(BF16)
        v_scr[0:seq] = v_ref[...].astype(BF16)
        v_scr[seq:] = vc_ref[...].astype(BF16)

    lam = _lambda(lam_ref, lam_init)
    q = _rope(q_ref[...].astype(F32), cq_ref[...], saq_ref[...], sbq_ref[...])
    _attn_core(q, k_scr[...], v_scr[...], lam, g_ref, o_ref, lam_init)


def _diff_attn_ctx(y, lam_vecs, subln_g, batch, seq, lam_init):
    n = y.shape[0]
    width = DIFF_HEADS * LANE
    wb = lambda col: col // width
    return pl.pallas_call(
        functools.partial(_attn_ctx_kernel, lam_init=lam_init),
        out_shape=jax.ShapeDtypeStruct((n, width), BF16),
        grid=(batch,),
        in_specs=[pl.BlockSpec((seq, width), lambda b: (b, wb(COL_QD))),
                  pl.BlockSpec((seq, width), lambda b: (b, wb(COL_KD))),
                  pl.BlockSpec((seq, width), lambda b: (b, wb(COL_VD))),
                  pl.BlockSpec((4, DIFF_DH), lambda b: (0, 0)),
                  pl.BlockSpec((1, LANE), lambda b: (0, 0))],
        out_specs=pl.BlockSpec((seq, width), lambda b: (b, 0)),
        compiler_params=_cparams(("parallel",)),
        name="diff_attn_ctx",
    )(y, y, y, lam_vecs, subln_g)


def _diff_attn_lat(y, k_cache, v_cache, tabs, lam_vecs, subln_g, batch, seq, past, tq, lam_init):
    n = y.shape[0]
    nq = seq // tq
    hb = lambda col: col // LANE
    cos_t, sin_a, sin_b = tabs
    qtab = pl.BlockSpec((tq, LANE), lambda b, h, i: (i, 0))
    ktab = pl.BlockSpec((seq, LANE), lambda b, h, i: (0, 0))
    return pl.pallas_call(
        functools.partial(_attn_lat_kernel, seq=seq, lam_init=lam_init),
        out_shape=jax.ShapeDtypeStruct((n, DIFF_HEADS * LANE), BF16),
        grid=(batch, DIFF_HEADS, nq),
        in_specs=[pl.BlockSpec((tq, LANE), lambda b, h, i: (b * nq + i, hb(COL_QD) + h)),
                  pl.BlockSpec((seq, LANE), lambda b, h, i: (b, hb(COL_KD) + h)),
                  pl.BlockSpec((seq, LANE), lambda b, h, i: (b, hb(COL_VD) + h)),
                  pl.BlockSpec((past, LANE), lambda b, h, i: (b, h)),
                  pl.BlockSpec((past, LANE), lambda b, h, i: (b, h)),
                  qtab, qtab, qtab, ktab, ktab, ktab,
                  pl.BlockSpec((4, DIFF_DH), lambda b, h, i: (0, 0)),
                  pl.BlockSpec((1, LANE), lambda b, h, i: (0, 0))],
        out_specs=pl.BlockSpec((tq, LANE), lambda b, h, i: (b * nq + i, h)),
        scratch_shapes=[pltpu.VMEM((seq + past, LANE), BF16), pltpu.VMEM((seq + past, LANE), BF16)],
        compiler_params=_cparams(("parallel", "parallel", "arbitrary")),
        name="diff_attn_lat",
    )(y, y, y, k_cache, v_cache, cos_t, sin_a, sin_b, cos_t, sin_a, sin_b, lam_vecs, subln_g)


def _rope_tables(seq):
    rows = seq // GRID_W
    pos_r = jnp.repeat(jnp.arange(rows, dtype=F32), GRID_W)
    pos_c = jnp.tile(jnp.arange(GRID_W, dtype=F32), rows)
    inv = ROPE_BASE ** (-jnp.arange(ROPE_P, dtype=F32) / ROPE_P)
    ang_r = pos_r[:, None] * inv
    ang_c = pos_c[:, None] * inv
    zeros = jnp.zeros((seq, ROPE_P), F32)

    def axis_tabs(ang):
        cs, sn = jnp.cos(ang), jnp.sin(ang)
        return (jnp.concatenate([cs, cs], -1), jnp.concatenate([-sn, zeros], -1), jnp.concatenate([zeros, sn], -1))

    tr, tc = axis_tabs(ang_r), axis_tabs(ang_c)
    reps = LANE // (2 * ROPE_AXIS_DIM)
    return tuple(jnp.tile(jnp.concatenate([a, b], -1), (1, reps)) for a, b in zip(tr, tc))


MERGE_SUB = 128


def _merge_kernel(of_ref, ob_ref, ra_ref, od_ref, ga_ref, gb_ref, x_ref, mod_ref, gn_ref, wpa_ref, wpb_ref,
                  wout_ref, l1g_ref, l1b_ref, wr_ref, x1_ref, h2_ref, aff_ref, *, alpha):
    mod = mod_ref[0]
    tm = x_ref.shape[0]
    sub = min(tm, MERGE_SUB)
    blocks = [slice(i * sub, (i + 1) * sub) for i in range(tm // sub)]

    def gla_out(rows):
        o = of_ref[rows, :] + ob_ref[rows, :]
        parts = []
        for h in range(GLA_HEADS):
            oh = o[:, h * GLA_DV:(h + 1) * GLA_DV]
            parts.append(oh * lax.rsqrt(jnp.mean(oh * oh, axis=-1, keepdims=True) + EPS) * gn_ref[...])
        return (jnp.concatenate(parts, axis=-1) * _silu(ra_ref[rows, :].astype(F32))).astype(BF16)

    o_as = [gla_out(rows) for rows in blocks]
    pas = [jnp.dot(o_a, wpa_ref[...], preferred_element_type=F32) for o_a in o_as]
    pbs = [jnp.dot(od_ref[rows, :], wpb_ref[...], preferred_element_type=F32) for rows in blocks]
    mergeds = [(_sigmoid(ga_ref[rows, :].astype(F32)) * pa + _sigmoid(gb_ref[rows, :].astype(F32)) * pb).astype(BF16)
               for rows, pa, pb in zip(blocks, pas, pbs)]
    mixes = [jnp.dot(merged, wout_ref[...], preferred_element_type=F32) for merged in mergeds]
    for rows, mix in zip(blocks, mixes):
        x1 = _ln_plain(alpha * x_ref[rows, :] + mod[2:3] * mix) * l1g_ref[...] + l1b_ref[...]
        x1_ref[rows, :] = x1
        h2 = _ln_plain(x1) * (1.0 + mod[4:5]) + mod[3:4]
        h2_ref[rows, :] = h2.astype(h2_ref.dtype)
        logits = lax.dot_general(wr_ref[...], h2, (((1,), (1,)), ((), ())), precision=HIGHEST,
                                 preferred_element_type=F32)
        e = jnp.exp(logits - jnp.max(logits, axis=0, keepdims=True))
        aff_ref[:, rows] = e / jnp.sum(e, axis=0, keepdims=True)


def _merge(o_f, o_b, y, o_d, x, mod_all, mod_row, gn, wpa, wpb, wout, l1g, l1b, wr_t, tm, alpha):
    n, d = x.shape
    cb = lambda col: col // d
    rowspec = lambda col: pl.BlockSpec((tm, d), lambda i: (i, col))
    const = lambda shape: pl.BlockSpec(shape, lambda i: (0,) * len(shape))
    return pl.pallas_call(
        functools.partial(_merge_kernel, alpha=alpha),
        out_shape=(jax.ShapeDtypeStruct((n, d), F32), jax.ShapeDtypeStruct((n, d), F32),
                   jax.ShapeDtypeStruct((N_EXPERTS, n), F32)),
        grid=(n // tm,),
        in_specs=[rowspec(0), rowspec(0),
                  rowspec(cb(COL_RA)), rowspec(0), rowspec(cb(COL_GA)), rowspec(cb(COL_GB)), rowspec(0),
                  pl.BlockSpec((1, 6, d), lambda i: (mod_row(i, tm), 0, 0)),
                  const((1, GLA_DV)), const((d, d)), const((d, d)), const((d, d)),
                  const((1, d)), const((1, d)), const((N_EXPERTS, d))],
        out_specs=(pl.BlockSpec((tm, d), lambda i: (i, 0)), pl.BlockSpec((tm, d), lambda i: (i, 0)),
                   pl.BlockSpec((N_EXPERTS, tm), lambda i: (0, i))),
        compiler_params=_cparams(("parallel",)),
        name="merge",
    )(o_f, o_b, y, o_d, y, y, x, mod_all, gn, wpa, wpb, wout, l1g, l1b, wr_t)


TM_ID, TM_ONE, TM_GHI, TM_GMID, TM_GLO, TM_POS = 0, 16, 32, 48, 64, 80
EXP_BISECT = 7
LIN_BISECT = 50


def _route_kernel(aff_ref, bind_ref, tm_ref, pos_ref, offb_ref, *, cap):
    n_exp, n = aff_ref.shape
    nb = n // LANE
    aff = aff_ref[...]
    capf = float(cap)

    def count_ge(thr):
        return jnp.sum(jnp.where(aff >= thr, 1.0, 0.0), axis=1, keepdims=True)

    def exp_step(_, c):
        klo, khi = c
        kmid = jnp.floor((klo + khi) * 0.5)
        ok = count_ge(jnp.exp2(kmid)) >= capf
        return jnp.where(ok, kmid, klo), jnp.where(ok, khi, kmid)

    klo, khi = lax.fori_loop(0, EXP_BISECT, exp_step,
                             (jnp.full((n_exp, 1), -127.0, F32), jnp.full((n_exp, 1), 1.0, F32)))
    lo0 = jnp.where(klo <= -127.0, 0.0, jnp.exp2(klo))
    hi0 = jnp.exp2(khi)

    def lin_step(_, c):
        lo, hi = c
        mid = lo + (hi - lo) * 0.5
        ok = count_ge(mid) >= capf
        return jnp.where(ok, mid, lo), jnp.where(ok, hi, mid)

    lo, hi = lax.fori_loop(0, LIN_BISECT, lin_step, (lo0, hi0))

    row = lax.broadcasted_iota(jnp.int32, (LANE, LANE), 0)
    col = lax.broadcasted_iota(jnp.int32, (LANE, LANE), 1)
    incl = jnp.where(row <= col, 1.0, 0.0).astype(BF16)
    excl = jnp.where(row < col, 1.0, 0.0).astype(BF16)
    bind = bind_ref[...]

    def prefix(mask):
        xb = jnp.where(mask, 1.0, 0.0).astype(BF16)
        stacked = jnp.concatenate([xb[:, j * LANE:(j + 1) * LANE] for j in range(nb)], axis=0)
        within = jnp.dot(stacked, incl, preferred_element_type=F32)
        within = jnp.concatenate([within[j * n_exp:(j + 1) * n_exp, :] for j in range(nb)], axis=1)
        block_tot = jnp.dot(xb, bind, preferred_element_type=F32)
        block_off = jnp.dot(block_tot.astype(BF16), excl, preferred_element_type=F32)
        off_hi = jnp.floor(block_off * (1.0 / LANE))
        off_lo = block_off - off_hi * LANE
        nt_dims = (((1,), (1,)), ((), ()))
        spread = (lax.dot_general(off_hi.astype(BF16), bind, nt_dims, preferred_element_type=F32) * LANE
                  + lax.dot_general(off_lo.astype(BF16), bind, nt_dims, preferred_element_type=F32))
        return within + spread, block_off

    above = aff >= hi
    tied = jnp.logical_and(aff >= lo, jnp.logical_not(above))
    need = capf - jnp.sum(jnp.where(above, 1.0, 0.0), axis=1, keepdims=True)
    tie_rank, _ = prefix(tied)
    sel = jnp.logical_or(above, jnp.logical_and(tied, tie_rank <= need))
    cum, block_off = prefix(sel)
    pos = jnp.where(sel, cum - 1.0, -1.0)
    pos_ref[...] = pos
    offb_ref[...] = block_off

    g_hi = aff.astype(BF16).astype(F32)
    rem = aff - g_hi
    g_mid = rem.astype(BF16).astype(F32)
    g_lo = (rem - g_mid).astype(BF16).astype(F32)
    lane_id = lax.broadcasted_iota(jnp.int32, (n_exp, LANE), 1).astype(F32)
    ones = jnp.ones((n_exp, LANE), F32)
    fill = jnp.zeros((LANE - 6 * n_exp, LANE), F32)
    for j in range(nb):
        cs = slice(j * LANE, (j + 1) * LANE)
        local = lane_id + float(LANE * (j % (COMBINE_TM // LANE)))
        fields = jnp.concatenate([local, ones, g_hi[:, cs], g_mid[:, cs], g_lo[:, cs], pos[:, cs], fill], axis=0)
        tm_ref[cs, :] = jnp.transpose(fields)


def _route(aff_t, cap):
    n_exp, n = aff_t.shape
    bind = (jnp.arange(n, dtype=jnp.int32)[:, None] // LANE == jnp.arange(LANE, dtype=jnp.int32)[None, :]).astype(BF16)
    return pl.pallas_call(
        functools.partial(_route_kernel, cap=cap),
        out_shape=(jax.ShapeDtypeStruct((n, LANE), F32), jax.ShapeDtypeStruct((n_exp, n), F32),
                   jax.ShapeDtypeStruct((n_exp, LANE), F32)),
        compiler_params=pltpu.CompilerParams(vmem_limit_bytes=VMEM_LIMIT),
        name="route",
    )(aff_t, bind)


def _compact_kernel(base_ref, npass_ref, pos_ref, tm_ref, acc_ref, *, cap):
    t = pl.program_id(0)
    nt = pl.num_programs(0)
    n_exp = pos_ref.shape[0]

    @pl.when(t == 0)
    def _():
        acc_ref[...] = jnp.zeros(acc_ref.shape, F32)

    pos = pos_ref[...]
    table = tm_ref[...].astype(BF16)
    lane = lax.broadcasted_iota(jnp.int32, (1, LANE), 1)
    tile_scale = jnp.where(jnp.logical_and(lane >= TM_ONE, lane < TM_GHI), t.astype(F32), 1.0)
    slot_iota = lax.broadcasted_iota(jnp.int32, (WIN, pos.shape[1]), 0).astype(F32)

    def one_pass(p, carry):
        onehot = jnp.concatenate(
            [jnp.where(pos[e:e + 1, :] - (base_ref[e * nt + t] + p * WIN).astype(F32) == slot_iota, 1.0, 0.0)
             for e in range(n_exp)], axis=0).astype(BF16)
        rec = jnp.dot(onehot, table, preferred_element_type=F32) * tile_scale
        for e in range(n_exp):
            start = pl.multiple_of(jnp.minimum(base_ref[e * nt + t] + p * WIN, cap), 8)
            mine = jnp.logical_and(lane % n_exp == e, lane < TM_POS)
            acc_ref[pl.ds(start, WIN), :] += jnp.where(mine, rec[e * WIN:(e + 1) * WIN, :], 0.0)
        return carry

    lax.fori_loop(0, npass_ref[t], one_pass, 0)


def _compact(pos, tm, base, npass, cap):
    n_exp, n = pos.shape
    tm_rows = COMBINE_TM
    grid_spec = pltpu.PrefetchScalarGridSpec(
        num_scalar_prefetch=2,
        grid=(n // tm_rows,),
        in_specs=[pl.BlockSpec((n_exp, tm_rows), lambda i, *_: (0, i)),
                  pl.BlockSpec((tm_rows, LANE), lambda i, *_: (i, 0))],
        out_specs=pl.BlockSpec((cap + WIN, LANE), lambda i, *_: (0, 0)))
    return pl.pallas_call(
        functools.partial(_compact_kernel, cap=cap),
        out_shape=jax.ShapeDtypeStruct((cap + WIN, LANE), F32),
        grid_spec=grid_spec,
        compiler_params=_cparams(("arbitrary",)),
        name="compact",
    )(base.reshape(-1), npass, pos, tm)


def _routing(aff_t, cap):
    n_exp, n = aff_t.shape
    tm, pos, offb = _route(aff_t, cap)
    blocks_per_tile = COMBINE_TM // LANE
    off = offb[:, :n // LANE].astype(jnp.int32)[:, ::blocks_per_tile]
    nxt = jnp.concatenate([off[:, 1:], jnp.full((n_exp, 1), cap, jnp.int32)], axis=1)
    base = (off // 8) * 8
    npass = jnp.maximum(jnp.max(-(-(nxt - base) // WIN), axis=0), 1).astype(jnp.int32)
    acc = _compact(pos, tm, base, npass, cap)[:cap]
    idx = (acc[:, TM_ID:TM_ID + n_exp] + COMBINE_TM * acc[:, TM_ONE:TM_ONE + n_exp]).T.astype(jnp.int32)
    gate = (acc[:, TM_GHI:TM_GHI + n_exp] + acc[:, TM_GMID:TM_GMID + n_exp] + acc[:, TM_GLO:TM_GLO + n_exp]).T
    return idx, gate, tm, base, npass


def _moe_kernel(idx_ref, hp_hbm, hs_hbm, gate_ref, wg_ref, wu_ref, wd_ref, yp_ref, ys_ref, xf_scr, xb_scr, sem,
                *, rows, cap_p, nf, s_tiles):
    e = pl.program_id(0)
    f = pl.program_id(1)
    n_exp = pl.num_programs(0)
    cap = xb_scr.shape[0]
    cap_s = cap - cap_p
    total = xf_scr.shape[0]

    def wait_rows(count):
        def drain(r, carry):
            pltpu.make_async_copy(hp_hbm.at[pl.ds(0, 1)], xf_scr.at[pl.ds(r, 1)], sem).wait()
            return carry
        lax.fori_loop(0, count, drain, 0, unroll=8)

    @pl.when(f == 0)
    def _():
        @pl.when(e == 0)
        def _():
            def issue_from(h_hbm):
                def issue(r, carry):
                    pltpu.make_async_copy(h_hbm.at[pl.ds(idx_ref[r], 1)], xf_scr.at[pl.ds(r, 1)], sem).start()
                    return carry
                return issue

            lax.fori_loop(0, cap_p, issue_from(hp_hbm), 0, unroll=8)
            lax.fori_loop(cap_p, cap, issue_from(hs_hbm), 0, unroll=8)
            wait_rows(cap)

        @pl.when(e > 0)
        def _():
            wait_rows(total)

        xb_scr[...] = xf_scr[0:cap, :].astype(BF16)

    nxt = jnp.minimum(e + 1, n_exp - 1) * cap

    def prefetch_group(h_hbm, group, n_groups, first_slot, spare_slot):
        real = group < n_groups
        src = first_slot + jnp.where(real, group, 0) * GATHER_G
        dst = jnp.where(real, first_slot + group * GATHER_G, spare_slot + (group - n_groups) * GATHER_G)
        for k in range(GATHER_G):
            pltpu.make_async_copy(h_hbm.at[pl.ds(idx_ref[nxt + src + k], 1)], xf_scr.at[pl.ds(dst + k, 1)],
                                  sem).start()

    p_groups, s_groups = cap_p // GATHER_G, cap_s // GATHER_G
    wg = wg_ref[0].astype(BF16)
    wu = wu_ref[0].astype(BF16)
    wd = wd_ref[0].astype(BF16)
    for t in range(cap // rows):
        if t == 0:
            prefetch_group(hp_hbm, f, p_groups, 0, cap)
        elif t <= s_tiles:
            prefetch_group(hs_hbm, f * s_tiles + (t - 1), s_groups, cap_p, cap + (nf - p_groups) * GATHER_G)
        lo = t * rows
        out_ref, olo = (yp_ref, lo) if lo < cap_p else (ys_ref, lo - cap_p)
        xr = xb_scr[lo:lo + rows, :]
        hg = jnp.dot(xr, wg, preferred_element_type=F32)
        hu = jnp.dot(xr, wu, preferred_element_type=F32)
        hid = (_silu(hg) * hu).astype(BF16)
        prev = jnp.where(f > 0, out_ref[0, olo:olo + rows, :], 0.0)
        out_ref[0, olo:olo + rows, :] = prev + jnp.dot(hid, wd, preferred_element_type=F32)

    @pl.when(jnp.logical_and(e == n_exp - 1, f == nf - 1))
    def _():
        wait_rows(total)

    @pl.when(f == nf - 1)
    def _():
        yp_ref[0, 0:cap_p, :] = yp_ref[0, 0:cap_p, :] * gate_ref[0, 0:cap_p, :]
        ys_ref[0, 0:cap_s, :] = ys_ref[0, 0:cap_s, :] * gate_ref[0, cap_p:, :]
        yp_ref[0, cap_p:, :] = jnp.zeros((WIN, yp_ref.shape[2]), F32)
        ys_ref[0, cap_s:, :] = jnp.zeros((WIN, ys_ref.shape[2]), F32)


def _moe(idx, hp, hs, gate, wg, wu, wd, cap_p, ft, rows):
    e, cap = idx.shape
    cap_s = cap - cap_p
    d = hp.shape[1]
    ff = wg.shape[2]
    nf = ff // ft
    p_groups, s_groups = cap_p // GATHER_G, cap_s // GATHER_G
    s_tiles = -(-s_groups // nf)
    assert cap_p % GATHER_G == 0 and cap_s % GATHER_G == 0 and p_groups <= nf and s_tiles <= cap // rows - 1
    gather_rows = nf * (1 + s_tiles) * GATHER_G
    grid_spec = pltpu.PrefetchScalarGridSpec(
        num_scalar_prefetch=1,
        grid=(e, ff // ft),
        in_specs=[pl.BlockSpec(memory_space=pl.ANY),
                  pl.BlockSpec(memory_space=pl.ANY),
                  pl.BlockSpec((1, cap, 1), lambda i, f, idx_ref: (i, 0, 0)),
                  pl.BlockSpec((1, d, ft), lambda i, f, idx_ref: (i, 0, f)),
                  pl.BlockSpec((1, d, ft), lambda i, f, idx_ref: (i, 0, f)),
                  pl.BlockSpec((1, ft, d), lambda i, f, idx_ref: (i, f, 0))],
        out_specs=(pl.BlockSpec((1, cap_p + WIN, d), lambda i, f, idx_ref: (i, 0, 0)),
                   pl.BlockSpec((1, cap_s + WIN, d), lambda i, f, idx_ref: (i, 0, 0))),
        scratch_shapes=[pltpu.VMEM((gather_rows, d), F32), pltpu.VMEM((cap, d), BF16), pltpu.SemaphoreType.DMA])
    return pl.pallas_call(
        functools.partial(_moe_kernel, rows=rows, cap_p=cap_p, nf=nf, s_tiles=s_tiles),
        out_shape=(jax.ShapeDtypeStruct((e, cap_p + WIN, d), F32), jax.ShapeDtypeStruct((e, cap_s + WIN, d), F32)),
        grid_spec=grid_spec,
        compiler_params=_cparams(("arbitrary", "arbitrary")),
        name="moe",
    )(idx.reshape(-1), hp, hs, gate, wg, wu, wd)


def _combine_kernel(base_ref, npass_ref, x1_ref, tm_ref, mod_ref, g_ref, b_ref, ye_hbm, o_ref,
                    ybuf, xbuf, sem, xsem, *, alpha):
    t = pl.program_id(0)
    nt = pl.num_programs(0)
    n_exp = ye_hbm.shape[0]
    tm = x1_ref.shape[0]
    cap = ye_hbm.shape[1] - WIN

    def window(tile, e, p, dst, s):
        start = pl.multiple_of(jnp.minimum(base_ref[e * nt + tile] + p * WIN, cap), 8)
        return pltpu.make_async_copy(ye_hbm.at[e, pl.ds(start, WIN), :], dst.at[pl.ds(e * WIN, WIN), :], s)

    def start_first_pass(tile, slot):
        for e in range(n_exp):
            window(tile, e, 0, ybuf.at[slot], sem.at[slot]).start()

    def expand(yw, p):
        lane = lax.broadcasted_iota(jnp.int32, (tm, LANE), 1).astype(F32)
        groups = []
        for gi in range(n_exp * WIN // LANE):
            ea, eb = 2 * gi, 2 * gi + 1
            ra = tm_ref[:, TM_POS + ea:TM_POS + ea + 1] - (base_ref[ea * nt + t] + p * WIN).astype(F32)
            rb = tm_ref[:, TM_POS + eb:TM_POS + eb + 1] - (base_ref[eb * nt + t] + (p - 1) * WIN).astype(F32)
            groups.append(jnp.where(lane == jnp.where(lane < WIN, ra, rb), 1.0, 0.0).astype(BF16))
        onehot = jnp.concatenate(groups, axis=1)
        hi = yw.astype(BF16)
        lo = (yw - hi.astype(F32)).astype(BF16)
        return (jnp.dot(onehot, hi, preferred_element_type=F32) + jnp.dot(onehot, lo, preferred_element_type=F32))

    @pl.when(t == 0)
    def _():
        start_first_pass(0, 0)

    slot = t % 2
    for e in range(n_exp):
        window(t, e, 0, ybuf.at[slot], sem.at[slot]).wait()

    @pl.when(t + 1 < nt)
    def _():
        start_first_pass(t + 1, 1 - slot)

    npass = npass_ref[t]

    def start_extra_pass(p):
        for e in range(n_exp):
            window(t, e, p, xbuf.at[p % 2], xsem.at[p % 2]).start()

    @pl.when(npass > 1)
    def _():
        start_extra_pass(1)

    acc = expand(ybuf[slot], 0)

    def extra_pass(p, acc):
        for e in range(n_exp):
            window(t, e, p, xbuf.at[p % 2], xsem.at[p % 2]).wait()

        @pl.when(p + 1 < npass)
        def _():
            start_extra_pass(p + 1)

        return acc + expand(xbuf[p % 2], p)

    acc = lax.fori_loop(1, npass, extra_pass, acc)
    mod = mod_ref[0]
    o_ref[...] = _ln_plain(alpha * x1_ref[...] + mod[5:6] * acc) * g_ref[...] + b_ref[...]


def _combine(x1, table, base, npass, ye, mod_all, mod_row, g, b, alpha):
    n, d = x1.shape
    n_exp = ye.shape[0]
    tm = COMBINE_TM
    grid_spec = pltpu.PrefetchScalarGridSpec(
        num_scalar_prefetch=2,
        grid=(n // tm,),
        in_specs=[pl.BlockSpec((tm, d), lambda i, *_: (i, 0)),
                  pl.BlockSpec((tm, LANE), lambda i, *_: (i, 0)),
                  pl.BlockSpec((1, 6, d), lambda i, *_: (mod_row(i, tm), 0, 0)),
                  pl.BlockSpec((1, d), lambda i, *_: (0, 0)), pl.BlockSpec((1, d), lambda i, *_: (0, 0)),
                  pl.BlockSpec(memory_space=pl.ANY)],
        out_specs=pl.BlockSpec((tm, d), lambda i, *_: (i, 0)),
        scratch_shapes=[pltpu.VMEM((2, n_exp * WIN, d), F32), pltpu.VMEM((2, n_exp * WIN, d), F32),
                        pltpu.SemaphoreType.DMA((2,)), pltpu.SemaphoreType.DMA((2,))])
    return pl.pallas_call(
        functools.partial(_combine_kernel, alpha=alpha),
        out_shape=jax.ShapeDtypeStruct((n, d), F32),
        grid_spec=grid_spec,
        compiler_params=_cparams(("arbitrary",)),
        name="combine",
    )(base.reshape(-1), npass, x1, table, mod_all, g, b, ye)


def _pick(n, pref):
    t = pref
    while n % t:
        t //= 2
    return t


def _mix_group(x3, mod_all, mod_row, lw, lam_init, alpha, s0, ctx):
    batch, seq, d = x3.shape
    n = batch * seq
    x = x3.reshape(n, d)
    tile_span = n if ctx is None else seq
    y_dtype = F32 if ctx is None else BF16
    y, a = _in_proj(x, mod_all, mod_row, lw["w_main"], lw["w_a"], _pick(tile_span, 1024), 1024, y_dtype)
    o_f, s_f = _gla(y, a, lw["w_alpha_pad"][0], lw["b_alpha"][0], s0[0], batch, seq, False)
    o_b, s_b = _gla(y, a, lw["w_alpha_pad"][1], lw["b_alpha"][1], s0[1], batch, seq, True)
    if ctx is None:
        o_d = _diff_attn_ctx(y, lw["lam_vecs"], lw["subln_g"], batch, seq, lam_init)
    else:
        k_cache, v_cache = ctx
        past = k_cache.shape[0] // batch
        o_d = _diff_attn_lat(y, k_cache, v_cache, _rope_tables(seq), lw["lam_vecs"], lw["subln_g"],
                             batch, seq, past, _pick(seq, 2048), lam_init)
    tm = _pick(seq, 512)
    x1, h2, aff_t = _merge(o_f, o_b, y, o_d, x, mod_all, mod_row, lw["gla_norm_g"], lw["w_proj_a"], lw["w_proj_b"],
                           lw["w_out"], lw["ln1_g"], lw["ln1_b"], lw["w_router_t"], tm, alpha)
    idx, gate, table, base, npass = _routing(aff_t, CAPACITY_FACTOR * n // N_EXPERTS)
    return dict(x1=x1, h2=h2, gate=gate, idx=idx, table=table, base=base, npass=npass, y=y, states=(s_f, s_b),
                mod_row=mod_row, shape=(batch, seq, d))


def _ffn_groups(gp, gs, mod_all, lw, alpha):
    cap_p, cap_s = gp["idx"].shape[1], gs["idx"].shape[1]
    idx = jnp.concatenate([gp["idx"], gs["idx"]], axis=1)
    gate = jnp.concatenate([gp["gate"], gs["gate"]], axis=1)[..., None]
    ye_p, ye_s = _moe(idx, gp["h2"], gs["h2"], gate, lw["w_exp_gate"], lw["w_exp_up"], lw["w_exp_down"],
                      cap_p, 256, _pick(math.gcd(cap_p, cap_s), 512))
    outs = []
    for g, ye in ((gp, ye_p), (gs, ye_s)):
        batch, seq, d = g["shape"]
        x2 = _combine(g["x1"], g["table"], g["base"], g["npass"], ye, mod_all, g["mod_row"],
                      lw["ln2_g"], lw["ln2_b"], alpha)
        outs.append(x2.reshape(batch, seq, d))
    return outs


def kernel(x_prompt, x_sample, cache_diff_k, cache_diff_v, state_gla_fwd, state_gla_bwd, c, c_ctx, w_mod, b_mod, w_in, w_alpha_f, b_alpha_f, w_alpha_b, b_alpha_b, gla_norm_g, w_proj_a, lambda_q1, lambda_k1, lambda_q2, lambda_k2, subln_g, w_proj_b, w_out, ln1_g, ln1_b, w_router, w_exp_gate, w_exp_up, w_exp_down, ln2_g, ln2_b):
    depth = w_in.shape[0]
    bp, sp, d = x_prompt.shape
    bs, ss, _ = x_sample.shape
    alpha = (2.0 * depth) ** 0.25
    xp, xs = x_prompt, x_sample

    n_mod = 1 + bs
    mod_rows = -(-n_mod // 8) * 8
    c_all = jnp.zeros((mod_rows, d), F32).at[0].set(c_ctx).at[1:n_mod].set(c)

    new_k, new_v, new_f, new_b = [], [], [], []
    for l in range(depth):
        lam_init = 0.8 - 0.6 * math.exp(-0.3 * l)
        mod_all = _modulation(c_all, w_mod[l], b_mod[l]).reshape(mod_rows, 6, d)
        wi = w_in[l]
        o = 0
        pieces = {}
        for name, width in zip(("qa", "ka", "va", "ra", "af", "ab", "qd", "kd", "vd", "ga", "gb"),
                               (GLA_QK, GLA_QK, GLA_VW, GLA_VW, GLA_RANK, GLA_RANK, d, d, d, d, d)):
            pieces[name] = wi[:, o:o + width]
            o += width
        w_main = jnp.concatenate([pieces[k] for k in ("qa", "ka", "va", "ra", "qd", "kd", "vd", "ga", "gb")],
                                 axis=1).astype(BF16)
        w_a = jnp.concatenate([pieces["af"], pieces["ab"], jnp.zeros((d, LANE - 2 * GLA_RANK), F32)],
                              axis=1).astype(BF16)
        w_alpha_pad = jnp.zeros((2, LANE, GLA_QK), F32)
        w_alpha_pad = w_alpha_pad.at[0, 0:GLA_RANK].set(w_alpha_f[l]).at[1, GLA_RANK:2 * GLA_RANK].set(w_alpha_b[l])
        lw = dict(
            w_main=w_main, w_a=w_a, w_alpha_pad=w_alpha_pad.astype(BF16),
            b_alpha=jnp.stack([b_alpha_f[l], b_alpha_b[l]])[:, None, :],
            lam_vecs=jnp.stack([lambda_q1[l], lambda_k1[l], lambda_q2[l], lambda_k2[l]]),
            subln_g=subln_g[l][None, :], gla_norm_g=gla_norm_g[l][None, :],
            w_proj_a=w_proj_a[l].astype(BF16), w_proj_b=w_proj_b[l].astype(BF16), w_out=w_out[l].astype(BF16),
            ln1_g=ln1_g[l][None, :], ln1_b=ln1_b[l][None, :], w_router_t=w_router[l].T,
            w_exp_gate=w_exp_gate[l], w_exp_up=w_exp_up[l], w_exp_down=w_exp_down[l],
            ln2_g=ln2_g[l][None, :], ln2_b=ln2_b[l][None, :])

        zeros = jnp.zeros((2, bp, GLA_HEADS, GLA_DK, GLA_DV), F32)
        gp = _mix_group(xp, mod_all, lambda i, tm: 0, lw, lam_init, alpha, zeros, None)
        y_p = gp["y"]
        new_k.append(y_p[:, COL_KD:COL_KD + d].reshape(bp, sp, DIFF_HEADS, 2, DIFF_DH))
        new_v.append(y_p[:, COL_VD:COL_VD + d].reshape(bp, sp, DIFF_HEADS, 2 * DIFF_DH))
        new_f.append(gp["states"][0])
        new_b.append(gp["states"][1])

        s0 = (state_gla_fwd[:, l], state_gla_bwd[:, l])
        past = cache_diff_k.shape[2]
        ctx = (cache_diff_k[:, l].reshape(bs * past, d), cache_diff_v[:, l].reshape(bs * past, d))
        gs = _mix_group(xs, mod_all, lambda i, tm: 1 + (i * tm) // ss, lw, lam_init, alpha, s0, ctx)
        xp, xs = _ffn_groups(gp, gs, mod_all, lw, alpha)

    return (xp, xs, jnp.stack(new_k, axis=1), jnp.stack(new_v, axis=1),
            jnp.stack(new_f, axis=1), jnp.stack(new_b, axis=1))
```
